```python
import functools
import jax, jax.numpy as jnp
from jax import lax
import numpy as np

D_MODEL = 1024
BATCH = 8
SEQ = 2048
DEPTH = 2
DEC_BATCH = 32
DEC_SEQ = 1
PAST_LEN = 16384
PAGE_SIZE = 128

HEAD_DIM = 64
SB_HEADS = 8
MOBA_HEADS = 8
SB_WIDTH = SB_HEADS * HEAD_DIM
MOBA_WIDTH = MOBA_HEADS * HEAD_DIM
D_FF = 2816
SB_Q_BLOCK = 128
MOBA_BLOCK = 256
MOBA_TOPK = 3
MOBA_Q_CHUNK = 32
NORM_EPS = 1e-6
IN_COLS = 3 * SB_WIDTH + 3 * MOBA_WIDTH + 2 * D_MODEL
IN_SPLITS = (SB_WIDTH, 2 * SB_WIDTH, 3 * SB_WIDTH, 3 * SB_WIDTH + MOBA_WIDTH,
             3 * SB_WIDTH + 2 * MOBA_WIDTH, 3 * SB_WIDTH + 3 * MOBA_WIDTH,
             3 * SB_WIDTH + 3 * MOBA_WIDTH + D_MODEL)

kernel_name = "stickbreak_moba_gated_hybrid_step"


def rms_norm(x, g):
    xf = x.astype(jnp.float32)
    y = xf * lax.rsqrt(jnp.mean(xf * xf, axis=-1, keepdims=True) + NORM_EPS)
    return (y * g.astype(jnp.float32)).astype(x.dtype)


def half_step_ffn(h, g_pre, g_post, w_up, w_down):
    u = rms_norm(h, g_pre)
    a, b = jnp.split(u @ w_up, 2, axis=-1)
    return h + 0.5 * rms_norm((jax.nn.silu(a) * b) @ w_down, g_post)


def alibi_slopes(n_heads):
    return jnp.asarray(2.0 ** (-8.0 * np.arange(1, n_heads + 1) / n_heads), dtype=jnp.float32)


def sb_weights(z, allowed):
    log_beta = jax.nn.log_sigmoid(z)
    log_keep = jnp.where(allowed, jax.nn.log_sigmoid(-z), 0.0)
    later = lax.cumsum(log_keep, axis=z.ndim - 1, reverse=True) - log_keep
    return jnp.where(allowed, jnp.exp(log_beta + later), 0.0)


def sb_prompt(q, k, v):
    B, S, H, dh = q.shape
    scale = dh ** -0.5
    nq = S // SB_Q_BLOCK
    q_blocks = q.reshape(B, nq, SB_Q_BLOCK, H, dh).transpose(1, 0, 3, 2, 4)
    k_pos = jnp.arange(S)

    def one_block(args):
        bi, qb = args
        t = bi * SB_Q_BLOCK + jnp.arange(SB_Q_BLOCK)
        z = jnp.einsum('bhtd,bkhd->bhtk', qb, k, preferred_element_type=jnp.float32) * scale
        a = sb_weights(z, k_pos[None, :] < t[:, None])
        return jnp.einsum('bhtk,bkhd->bhtd', a, v.astype(jnp.float32))

    o = lax.map(one_block, (jnp.arange(nq), q_blocks))
    return o.transpose(1, 0, 3, 2, 4).reshape(B, S, H * dh)


def sb_sample(q, k_new, v_new, pool_k, pool_v, page_table):
    DB, T, H, dh = q.shape
    past = page_table.shape[1] * PAGE_SIZE
    scale = dh ** -0.5
    k_past = pool_k[page_table].reshape(DB, past, H, dh)
    v_past = pool_v[page_table].reshape(DB, past, H, dh)
    z_past = jnp.einsum('bthd,bkhd->bhtk', q, k_past, preferred_element_type=jnp.float32)
    z_new = jnp.einsum('bthd,bkhd->bhtk', q, k_new, preferred_element_type=jnp.float32)
    z = jnp.concatenate([z_past, z_new], axis=-1) * scale
    ti = jnp.arange(T)
    allowed = jnp.concatenate([jnp.ones((T, past), bool), ti[None, :] < ti[:, None]], axis=-1)
    a = sb_weights(z, allowed)
    o = (jnp.einsum('bhtk,bkhd->bhtd', a[..., :past], v_past.astype(jnp.float32))
         + jnp.einsum('bhtk,bkhd->bhtd', a[..., past:], v_new.astype(jnp.float32)))
    return o.transpose(0, 2, 1, 3).reshape(DB, T, H * dh)


def alibi_block_softmax(q_pos, slopes, own, sel=None):
    def biased(lg, k_pos, ok):
        dist = (q_pos[:, None] - k_pos).astype(jnp.float32)
        return jnp.where(ok, lg - slopes[:, None, None] * dist, -jnp.inf)
    lg_own = biased(*own[:3])
    if sel is None:
        p_own = jax.nn.softmax(lg_own, axis=-1)
        return jnp.einsum('bhtk,bhkd->bhtd', p_own, own[3].astype(jnp.float32))
    lg_sel = biased(*sel[:3])
    ks = lg_sel.shape[-1]
    p = jax.nn.softmax(jnp.concatenate([lg_sel, lg_own], axis=-1), axis=-1)
    return (jnp.einsum('bhtk,bhtkd->bhtd', p[..., :ks], sel[3].astype(jnp.float32))
            + jnp.einsum('bhtk,bhkd->bhtd', p[..., ks:], own[3].astype(jnp.float32)))


def moba_prompt(q, k, v, slopes):
    B, S, H, dh = q.shape
    scale = dh ** -0.5
    nb = -(-S // MOBA_BLOCK)
    pad = nb * MOBA_BLOCK - S
    kb = jnp.pad(k, ((0, 0), (0, pad), (0, 0), (0, 0))).reshape(B, nb, MOBA_BLOCK, H, dh)
    vb = jnp.pad(v, ((0, 0), (0, pad), (0, 0), (0, 0))).reshape(B, nb, MOBA_BLOCK, H, dh)
    k_mean = jnp.mean(kb, axis=2, dtype=jnp.float32)
    n_sel = min(MOBA_TOPK, nb - 1)
    nq = S // MOBA_Q_CHUNK
    q_chunks = q.reshape(B, nq, MOBA_Q_CHUNK, H, dh).transpose(1, 0, 3, 2, 4)
    blk_off = jnp.arange(MOBA_BLOCK)
    bi = jnp.arange(B)[:, None, None, None, None]
    hi = jnp.arange(H)[None, :, None, None, None]

    def one_chunk(args):
        ci, qc = args
        t = ci * MOBA_Q_CHUNK + jnp.arange(MOBA_Q_CHUNK)
        own = (ci * MOBA_Q_CHUNK) // MOBA_BLOCK
        k_own = lax.dynamic_index_in_dim(kb, own, axis=1, keepdims=False)
        v_own = lax.dynamic_index_in_dim(vb, own, axis=1, keepdims=False)
        pos_own = own * MOBA_BLOCK + blk_off
        lg_own = jnp.einsum('bhtd,bkhd->bhtk', qc, k_own, preferred_element_type=jnp.float32) * scale
        own_part = (lg_own, pos_own, pos_own[None, :] <= t[:, None], v_own.transpose(0, 2, 1, 3))
        if n_sel == 0:
            return alibi_block_softmax(t, slopes, own_part)
        sc = jnp.einsum('bhtd,bnhd->bhtn', qc, k_mean, preferred_element_type=jnp.float32)
        sc = jnp.where(jnp.arange(nb) < own, sc, -jnp.inf)
        _, idx = lax.top_k(sc, n_sel)
        ib = idx[..., None]
        k_sel = kb[bi, ib, blk_off, hi].reshape(B, H, MOBA_Q_CHUNK, n_sel * MOBA_BLOCK, dh)
        v_sel = vb[bi, ib, blk_off, hi].reshape(B, H, MOBA_Q_CHUNK, n_sel * MOBA_BLOCK, dh)
        pos_sel = (ib * MOBA_BLOCK + blk_off).reshape(B, H, MOBA_Q_CHUNK, n_sel * MOBA_BLOCK)
        ok_sel = jnp.repeat(jnp.arange(n_sel) < own, MOBA_BLOCK)
        lg_sel = jnp.einsum('bhtd,bhtkd->bhtk', qc, k_sel, preferred_element_type=jnp.float32) * scale
        return alibi_block_softmax(t, slopes, own_part, (lg_sel, pos_sel, ok_sel, v_sel))

    o = lax.map(one_chunk, (jnp.arange(nq), q_chunks))
    return o.transpose(1, 0, 3, 2, 4).reshape(B, S, H * dh)


def moba_sample(q, k_new, v_new, pool_k, pool_v, page_table, slopes):
    DB, T, H, dh = q.shape
    n_pages = page_table.shape[1]
    past = n_pages * PAGE_SIZE
    scale = dh ** -0.5
    ppb = MOBA_BLOCK // PAGE_SIZE
    n_full = past // MOBA_BLOCK
    q_pos = past + jnp.arange(T)
    qh = q.transpose(0, 2, 1, 3)
    tail_phys = page_table[:, n_full * ppb:]
    k_tail = pool_k[tail_phys].reshape(DB, tail_phys.shape[1] * PAGE_SIZE, H, dh)
    v_tail = pool_v[tail_phys].reshape(DB, tail_phys.shape[1] * PAGE_SIZE, H, dh)
    k_own = jnp.concatenate([k_tail, k_new], axis=1)
    v_own = jnp.concatenate([v_tail, v_new], axis=1)
    pos_own = n_full * MOBA_BLOCK + jnp.arange(k_own.shape[1])
    lg_own = jnp.einsum('bhtd,bkhd->bhtk', qh, k_own, preferred_element_type=jnp.float32) * scale
    own_part = (lg_own, pos_own, pos_own[None, :] <= q_pos[:, None], v_own.transpose(0, 2, 1, 3))
    n_sel = min(MOBA_TOPK, n_full)
    if n_sel == 0:
        o = alibi_block_softmax(q_pos, slopes, own_part)
        return o.transpose(0, 2, 1, 3).reshape(DB, T, H * dh)
    k_full = pool_k[page_table[:, :n_full * ppb]].reshape(DB, n_full, MOBA_BLOCK, H, dh)
    k_mean = jnp.mean(k_full, axis=2, dtype=jnp.float32)
    sc = jnp.einsum('bhtd,bnhd->bhtn', qh, k_mean, preferred_element_type=jnp.float32)
    _, idx = lax.top_k(sc, n_sel)
    logical = idx[..., None] * ppb + jnp.arange(ppb)
    phys = page_table[jnp.arange(DB)[:, None, None, None, None], logical]
    hi = jnp.arange(H)[None, :, None, None, None, None]
    pidx = phys[..., None]
    row = jnp.arange(PAGE_SIZE)
    k_sel = pool_k[pidx, row, hi].reshape(DB, H, T, n_sel * MOBA_BLOCK, dh)
    v_sel = pool_v[pidx, row, hi].reshape(DB, H, T, n_sel * MOBA_BLOCK, dh)
    pos_sel = (idx[..., None] * MOBA_BLOCK + jnp.arange(MOBA_BLOCK)).reshape(DB, H, T, n_sel * MOBA_BLOCK)
    ok_sel = jnp.ones((n_sel * MOBA_BLOCK,), bool)
    lg_sel = jnp.einsum('bhtd,bhtkd->bhtk', qh, k_sel, preferred_element_type=jnp.float32) * scale
    o = alibi_block_softmax(q_pos, slopes, own_part, (lg_sel, pos_sel, ok_sel, v_sel))
    return o.transpose(0, 2, 1, 3).reshape(DB, T, H * dh)


def prompt_attend(qa, ka, va, qm, km, vm, slopes):
    return sb_prompt(qa, ka, va), moba_prompt(qm, km, vm, slopes)


def sample_attend(qa, ka, va, qm, km, vm, slopes, sb_k, sb_v, mb_k, mb_v, page_table):
    return (sb_sample(qa, ka, va, sb_k, sb_v, page_table),
            moba_sample(qm, km, vm, mb_k, mb_v, page_table, slopes))


def split_heads(t, n_heads):
    return t.reshape(t.shape[:-1] + (n_heads, HEAD_DIM))


def hybrid_layer(x, attend, g1_pre, g1_post, w1_up, w1_down, gm_pre, gm_post, w_in, b_gate,
                 w_branch_sb, w_branch_moba, w_out, g2_pre, g2_post, w2_up, w2_down):
    h = half_step_ffn(x, g1_pre, g1_post, w1_up, w1_down)
    u = rms_norm(h, gm_pre)
    qa, ka, va, qm, km, vm, g_sb, g_m = jnp.split(u @ w_in, IN_SPLITS, axis=-1)
    qa, ka, va = split_heads(qa, SB_HEADS), split_heads(ka, SB_HEADS), split_heads(va, SB_HEADS)
    qm, km, vm = split_heads(qm, MOBA_HEADS), split_heads(km, MOBA_HEADS), split_heads(vm, MOBA_HEADS)
    o_sb, o_m = attend(qa, ka, va, qm, km, vm)
    b_sb, b_m = jnp.split(b_gate, 2)
    merged = (jax.nn.sigmoid(g_sb + b_sb) * (o_sb.astype(x.dtype) @ w_branch_sb)
              + jax.nn.sigmoid(g_m + b_m) * (o_m.astype(x.dtype) @ w_branch_moba))
    h = h + rms_norm(merged @ w_out, gm_post)
    y = half_step_ffn(h, g2_pre, g2_post, w2_up, w2_down)
    return y, ka, va, km, vm


def setup_inputs(seed: int = 0) -> dict:
    key = jax.random.key(seed)
    ks = jax.random.split(key, 24)
    f32 = jnp.float32
    n_pages = PAST_LEN // PAGE_SIZE
    n_pool = (DEC_BATCH * n_pages * 5) // 4

    def nrm(k, shape, scale):
        return jax.random.normal(k, shape, f32) * scale

    def gain(k):
        return 1.0 + 0.05 * jax.random.normal(k, (DEPTH, D_MODEL), f32)

    page_table = jax.random.permutation(ks[6], n_pool)[:DEC_BATCH * n_pages]
    page_table = page_table.reshape(DEC_BATCH, n_pages).astype(jnp.int32)
    pool_shape = (DEPTH, n_pool, PAGE_SIZE, SB_HEADS, HEAD_DIM)
    mpool_shape = (DEPTH, n_pool, PAGE_SIZE, MOBA_HEADS, HEAD_DIM)
    return {
        "x_prompt": nrm(ks[0], (BATCH, SEQ, D_MODEL), 1.0),
        "x_sample": nrm(ks[1], (DEC_BATCH, DEC_SEQ, D_MODEL), 1.0),
        "cache_sb_k": nrm(ks[2], pool_shape, 1.0),
        "cache_sb_v": nrm(ks[3], pool_shape, 1.0),
        "cache_moba_k": nrm(ks[4], mpool_shape, 1.0),
        "cache_moba_v": nrm(ks[5], mpool_shape, 1.0),
        "page_table": page_table,
        "ffn1_norm_pre": gain(ks[7]),
        "ffn1_norm_post": gain(ks[8]),
        "ffn1_w_up": nrm(ks[9], (DEPTH, D_MODEL, 2 * D_FF), D_MODEL ** -0.5),
        "ffn1_w_down": nrm(ks[10], (DEPTH, D_FF, D_MODEL), D_FF ** -0.5),
        "mix_norm_pre": gain(ks[11]),
        "mix_norm_post": gain(ks[12]),
        "w_in": nrm(ks[13], (DEPTH, D_MODEL, IN_COLS), D_MODEL ** -0.5),
        "b_gate": nrm(ks[14], (DEPTH, 2 * D_MODEL), 0.1),
        "w_branch_sb": nrm(ks[15], (DEPTH, SB_WIDTH, D_MODEL), SB_WIDTH ** -0.5),
        "w_branch_moba": nrm(ks[16], (DEPTH, MOBA_WIDTH, D_MODEL), MOBA_WIDTH ** -0.5),
        "w_out": nrm(ks[17], (DEPTH, D_MODEL, D_MODEL), D_MODEL ** -0.5),
        "ffn2_norm_pre": gain(ks[18]),
        "ffn2_norm_post": gain(ks[19]),
        "ffn2_w_up": nrm(ks[20], (DEPTH, D_MODEL, 2 * D_FF), D_MODEL ** -0.5),
        "ffn2_w_down": nrm(ks[21], (DEPTH, D_FF, D_MODEL), D_FF ** -0.5),
    }


def reference(x_prompt, x_sample, cache_sb_k, cache_sb_v, cache_moba_k, cache_moba_v, page_table,
              ffn1_norm_pre, ffn1_norm_post, ffn1_w_up, ffn1_w_down,
              mix_norm_pre, mix_norm_post, w_in, b_gate, w_branch_sb, w_branch_moba, w_out,
              ffn2_norm_pre, ffn2_norm_post, ffn2_w_up, ffn2_w_down):
    slopes = alibi_slopes(MOBA_HEADS)
    y_prompt, y_sample = x_prompt, x_sample
    p_rows = ([], [], [], [])
    s_rows = ([], [], [], [])
    for l in range(DEPTH):
        w = (ffn1_norm_pre[l], ffn1_norm_post[l], ffn1_w_up[l], ffn1_w_down[l],
             mix_norm_pre[l], mix_norm_post[l], w_in[l], b_gate[l], w_branch_sb[l], w_branch_moba[l],
             w_out[l], ffn2_norm_pre[l], ffn2_norm_post[l], ffn2_w_up[l], ffn2_w_down[l])
        y_prompt, *rows_p = hybrid_layer(y_prompt, functools.partial(prompt_attend, slopes=slopes), *w)
        attend_s = functools.partial(sample_attend, slopes=slopes, sb_k=cache_sb_k[l], sb_v=cache_sb_v[l],
                                     mb_k=cache_moba_k[l], mb_v=cache_moba_v[l], page_table=page_table)
        y_sample, *rows_s = hybrid_layer(y_sample, attend_s, *w)
        for acc, r in zip(p_rows, rows_p):
            acc.append(r)
        for acc, r in zip(s_rows, rows_s):
            acc.append(r)
    sb_k_prompt = jnp.stack(p_rows[0])
    sb_v_prompt = jnp.stack(p_rows[1])
    moba_k_prompt = jnp.stack(p_rows[2])
    moba_v_prompt = jnp.stack(p_rows[3])
    sb_k_sample = jnp.stack(s_rows[0])
    sb_v_sample = jnp.stack(s_rows[1])
    moba_k_sample = jnp.stack(s_rows[2])
    moba_v_sample = jnp.stack(s_rows[3])
    return (y_prompt, y_sample, sb_k_prompt, sb_v_prompt, moba_k_prompt, moba_v_prompt,
            sb_k_sample, sb_v_sample, moba_k_sample, moba_v_sample)
```

```python
import functools

import numpy as np
import jax
import jax.numpy as jnp
from jax import lax
from jax.experimental import pallas as pl
from jax.experimental.pallas import tpu as pltpu

HEAD_DIM = 64
N_HEADS = 8
WIDTH = N_HEADS * HEAD_DIM
PAIR = 2 * HEAD_DIM
N_PAIRS = N_HEADS // 2
MOBA_BLOCK = 256
MOBA_TOPK = 3
PAGE_SIZE = 128
NORM_EPS = 1e-6
LANES = 128
VMEM_LIMIT = 56 * 1024 * 1024

F32 = jnp.float32
BF16 = jnp.bfloat16
NEG_INF = float("-inf")
NT_DIMS = (((1,), (1,)), ((), ()))


def _params(*semantics):
    return pltpu.CompilerParams(dimension_semantics=semantics, vmem_limit_bytes=VMEM_LIMIT)


def _rms(x, g):
    return x * lax.rsqrt(jnp.mean(x * x, axis=-1, keepdims=True) + NORM_EPS) * g


def _resident(shape, index_map):
    return pl.BlockSpec(shape, index_map, pipeline_mode=pl.Buffered(1))


FFN_CHUNK = 256


def _dot(a, b, dims=None):
    precision = lax.Precision.HIGHEST if a.dtype == F32 else None
    if dims is None:
        return jnp.dot(a, b, preferred_element_type=F32, precision=precision)
    return lax.dot_general(a, b, dims, preferred_element_type=F32, precision=precision)


def _ffn_kernel(x_ref, gpre_ref, gpost_ref, wa_ref, wb_ref, wd_ref, o_ref, acc_ref, u_ref):
    j = pl.program_id(1)

    @pl.when(j == 0)
    def _():
        u_ref[...] = _rms(x_ref[...], gpre_ref[...]).astype(u_ref.dtype)
        acc_ref[...] = jnp.zeros_like(acc_ref)

    u = u_ref[...]
    for c in range(wd_ref.shape[0] // FFN_CHUNK):
        cols = slice(c * FFN_CHUNK, (c + 1) * FFN_CHUNK)
        a = _dot(u, wa_ref[:, cols])
        b = _dot(u, wb_ref[:, cols])
        mid = (a * jax.nn.sigmoid(a) * b).astype(u.dtype)
        acc_ref[...] += _dot(mid, wd_ref[cols, :])

    @pl.when(j == pl.num_programs(1) - 1)
    def _():
        o_ref[...] = x_ref[...] + 0.5 * _rms(acc_ref[...], gpost_ref[...])


def _ffn(x, g_pre, g_post, w_up, w_down):
    m, d = x.shape
    f = w_down.shape[0]
    assert f % FFN_CHUNK == 0 and w_up.shape == (d, 2 * f)
    tm = min(512, m)
    assert m % tm == 0
    if w_up.dtype == BF16:
        tf, spec = f, _resident
    else:
        tf, spec = FFN_CHUNK, pl.BlockSpec
    n_slabs = f // tf
    return pl.pallas_call(
        _ffn_kernel,
        grid=(m // tm, n_slabs),
        in_specs=[
            pl.BlockSpec((tm, d), lambda i, j: (i, 0)),
            _resident((1, d), lambda i, j: (0, 0)),
            _resident((1, d), lambda i, j: (0, 0)),
            spec((d, tf), lambda i, j: (0, j)),
            spec((d, tf), lambda i, j: (0, n_slabs + j)),
            spec((tf, d), lambda i, j: (j, 0)),
        ],
        out_specs=pl.BlockSpec((tm, d), lambda i, j: (i, 0)),
        out_shape=jax.ShapeDtypeStruct((m, d), F32),
        scratch_shapes=[pltpu.VMEM((tm, d), F32), pltpu.VMEM((tm, d), w_up.dtype)],
        compiler_params=_params("parallel", "arbitrary"),
        name="ffn",
    )(x, g_pre.reshape(1, d), g_post.reshape(1, d), w_up, w_up, w_down)


def _in_kernel(*refs, prompt, n_carried):
    h_ref, g_ref, w_ref = refs[:3]
    outs = refs[3 + n_carried:]
    qa_ref, ka_ref, va_ref, qm_ref, km_ref, vm_ref, gate_ref = outs[:7]
    if prompt:
        kmean_ref, kat_ref, vat_ref, kmt_ref, vmt_ref = outs[7:]
    else:
        kat_ref = vat_ref = kmt_ref = vmt_ref = None
    u = _rms(h_ref[...], g_ref[...]).astype(w_ref.dtype)
    scale = HEAD_DIM ** -0.5

    def proj(c):
        return _dot(u, w_ref[:, c * WIDTH:(c + 1) * WIDTH])

    def emit(c, row_ref, t_ref):
        r = proj(c)
        row_ref[...] = r.astype(row_ref.dtype)
        if prompt:
            t_ref[...] = r.T
        return r

    qa_ref[...] = (proj(0) * scale).astype(qa_ref.dtype)
    emit(1, ka_ref, kat_ref)
    emit(2, va_ref, vat_ref)
    qm_ref[...] = (proj(3) * scale).astype(qm_ref.dtype)
    r = emit(4, km_ref, kmt_ref)
    if prompt:
        for j in range(r.shape[0] // MOBA_BLOCK):
            blk = r[j * MOBA_BLOCK:(j + 1) * MOBA_BLOCK]
            kmean_ref[j] = jnp.sum(blk, axis=0, keepdims=True) * (1.0 / MOBA_BLOCK)
    emit(5, vm_ref, vmt_ref)
    for c in range(gate_ref.shape[1] // WIDTH):
        gate_ref[:, c * WIDTH:(c + 1) * WIDTH] = proj(6 + c)


def _in_proj(h, g, w_in, *, layer=0, depth=1, batch=None, carried=()):
    m, d = h.shape
    n_cols = w_in.shape[1]
    assert n_cols == 6 * WIDTH + 2 * d
    prompt = batch is not None
    tm = min(512, m)
    assert m % tm == 0
    row = lambda i: (i, 0)
    wide = pl.BlockSpec((tm, WIDTH), row)
    kv_dtype = BF16 if prompt else F32
    out_specs = [wide] * 6 + [pl.BlockSpec((tm, 2 * d), row)]
    out_shape = [jax.ShapeDtypeStruct((m, WIDTH), kv_dtype)] * 6
    out_shape.append(jax.ShapeDtypeStruct((m, 2 * d), F32))
    aliases = {}
    if prompt:
        seq = m // batch
        assert seq % tm == 0 and tm % MOBA_BLOCK == 0
        tiles = seq // tm
        out_specs.append(pl.BlockSpec((tm // MOBA_BLOCK, 1, WIDTH), lambda i: (i, 0, 0)))
        out_shape.append(jax.ShapeDtypeStruct((m // MOBA_BLOCK, 1, WIDTH), F32))
        out_specs += [pl.BlockSpec((None, None, WIDTH, tm), lambda i: (layer, i // tiles, 0, i % tiles))] * 4
        out_shape += [jax.ShapeDtypeStruct((depth, batch, WIDTH, seq), F32)] * 4
        assert len(carried) == 4
        aliases = {3 + j: 8 + j for j in range(4)}
    return pl.pallas_call(
        functools.partial(_in_kernel, prompt=prompt, n_carried=len(carried)),
        grid=(m // tm,),
        in_specs=[
            pl.BlockSpec((tm, d), row),
            _resident((1, d), lambda i: (0, 0)),
            _resident((d, n_cols), lambda i: (0, 0)),
        ] + [pl.BlockSpec(memory_space=pl.ANY)] * len(carried),
        out_specs=out_specs,
        out_shape=out_shape,
        input_output_aliases=aliases,
        compiler_params=_params("parallel"),
        name="in_proj",
    )(h, g.reshape(1, d), w_in, *carried)


def _merge_kernel(h_ref, osb_ref, om_ref, gate_ref, bg_ref, wbs_ref, wbm_ref, wo_ref, gpost_ref, o_ref):
    d = h_ref.shape[1]
    a = _dot(osb_ref[...], wbs_ref[...])
    b = _dot(om_ref[...], wbm_ref[...])
    merged = (jax.nn.sigmoid(gate_ref[:, :d] + bg_ref[:, :d]) * a
              + jax.nn.sigmoid(gate_ref[:, d:] + bg_ref[:, d:]) * b)
    y = _dot(merged.astype(wo_ref.dtype), wo_ref[...])
    o_ref[...] = h_ref[...] + _rms(y, gpost_ref[...])


def _merge(h, o_sb, o_m, gates, b_gate, w_bs, w_bm, w_out, g_post):
    m, d = h.shape
    tm = min(512, m)
    row = lambda i: (i, 0)
    const = lambda i: (0, 0)
    return pl.pallas_call(
        _merge_kernel,
        grid=(m // tm,),
        in_specs=[
            pl.BlockSpec((tm, d), row),
            pl.BlockSpec((tm, WIDTH), row),
            pl.BlockSpec((tm, WIDTH), row),
            pl.BlockSpec((tm, 2 * d), row),
            _resident((1, 2 * d), const),
            _resident((WIDTH, d), const),
            _resident((WIDTH, d), const),
            _resident((d, d), const),
            _resident((1, d), const),
        ],
        out_specs=pl.BlockSpec((tm, d), row),
        out_shape=jax.ShapeDtypeStruct((m, d), F32),
        compiler_params=_params("parallel"),
        name="merge",
    )(h, o_sb, o_m, gates, b_gate.reshape(1, 2 * d), w_bs, w_bm, w_out, g_post.reshape(1, d))


def _suffix_matrix(n):
    j = np.arange(n)[:, None]
    s = np.arange(n)[None, :]
    return jnp.asarray(np.concatenate([(j > s), np.ones((n, n), bool)], axis=1), dtype=BF16)


def _log_sigmoids(z):
    log_beta = jnp.minimum(z, 0.0) - jnp.log(1.0 + jnp.exp(-jnp.abs(z)))
    return log_beta, log_beta - z


def _suffix_sums(x, uo):
    hi = x.astype(BF16)
    lo = (x - hi.astype(F32)).astype(BF16)
    t = jnp.dot(hi, uo, preferred_element_type=F32) + jnp.dot(lo, uo, preferred_element_type=F32)
    n = x.shape[1]
    return t[:, :n], t[:, n:]


SB_TILE = 128


def _sb_prompt_kernel(q_ref, k_ref, v_ref, uo_ref, o_ref):
    i = pl.program_id(2)
    t = SB_TILE
    q = q_ref[...]
    uo = uo_ref[...]
    lane = lax.broadcasted_iota(jnp.int32, (1, PAIR), 1)
    below_diag = (lax.broadcasted_iota(jnp.int32, (t, t), 1) < lax.broadcasted_iota(jnp.int32, (t, t), 0))
    out = jnp.zeros((t, PAIR), F32)
    for hh in range(2):
        head_lanes = (lane // HEAD_DIM) == hh
        qh = jnp.where(head_lanes, q, jnp.zeros_like(q))

        def block(j, carry, acc, diagonal):
            off = pl.multiple_of(j * t, t)
            z = lax.dot_general(qh, k_ref[pl.ds(off, t), :], NT_DIMS, preferred_element_type=F32)
            log_beta, log_keep = _log_sigmoids(z)
            if diagonal:
                log_keep = jnp.where(below_diag, log_keep, 0.0)
            later, total = _suffix_sums(log_keep, uo)
            a = jnp.exp(log_beta + later + carry)
            if diagonal:
                a = jnp.where(below_diag, a, 0.0)
            acc = acc + jnp.dot(a.astype(BF16), v_ref[pl.ds(off, t), :], preferred_element_type=F32)
            return carry + total, acc

        zero = jnp.zeros((t, t), F32)
        carry, acc = block(i, zero, jnp.zeros((t, PAIR), F32), True)
        carry, acc = lax.fori_loop(0, i, lambda s, c: block(i - 1 - s, c[0], c[1], False), (carry, acc))
        out = jnp.where(head_lanes, acc, out)
    o_ref[...] = out.astype(BF16)


def _sb_prompt(q, kb, vb, batch, seq):
    t = SB_TILE
    nq = seq // t
    assert seq % t == 0
    return pl.pallas_call(
        _sb_prompt_kernel,
        grid=(batch, N_PAIRS, nq),
        in_specs=[
            pl.BlockSpec((t, PAIR), lambda b, p, i: (b * nq + i, p)),
            pl.BlockSpec((seq, PAIR), lambda b, p, i: (b, p)),
            pl.BlockSpec((seq, PAIR), lambda b, p, i: (b, p)),
            _resident((t, 2 * t), lambda b, p, i: (0, 0)),
        ],
        out_specs=pl.BlockSpec((t, PAIR), lambda b, p, i: (b * nq + i, p)),
        out_shape=jax.ShapeDtypeStruct((batch * seq, WIDTH), BF16),
        compiler_params=_params("parallel", "parallel", "parallel"),
        name="sb_prompt",
    )(q, kb, vb, _suffix_matrix(t))


def _alibi_slopes():
    return jnp.asarray(2.0 ** (-8.0 * np.arange(1, N_HEADS + 1) / N_HEADS), dtype=F32)


def _moba_prompt_kernel(slopes_ref, q_ref, k_ref, v_ref, kma_ref, kmb_ref, o_ref, *, nb):
    p = pl.program_id(1)
    i = pl.program_id(2)
    t = MOBA_BLOCK
    q = q_ref[...]
    lane = lax.broadcasted_iota(jnp.int32, (1, PAIR), 1)
    row = lax.broadcasted_iota(jnp.int32, (t, t), 0)
    col = lax.broadcasted_iota(jnp.int32, (t, t), 1)
    row_minus_col = (row - col).astype(F32)
    causal = col <= row
    pl_lane = lax.broadcasted_iota(jnp.int32, (t, LANES), 1)
    m_of = pl_lane // nb
    n_of = pl_lane % nb
    pair_ok = (pl_lane < nb * nb) & (m_of < i)
    gl = lax.broadcasted_iota(jnp.int32, (LANES, LANES), 0)
    gn = lax.broadcasted_iota(jnp.int32, (LANES, LANES), 1)
    group = jnp.where((gl < nb * nb) & (gl % nb == gn), 1.0, 0.0).astype(BF16)
    pick_row = lax.broadcasted_iota(jnp.int32, (LANES, t), 0)
    out = jnp.zeros((t, PAIR), F32)
    for hh in range(2):
        head_lanes = (lane // HEAD_DIM) == hh
        qh = jnp.where(head_lanes, q, jnp.zeros_like(q))
        slope = slopes_ref[2 * p + hh]
        sa = lax.dot_general(qh, kma_ref[...], NT_DIMS, preferred_element_type=F32)
        sb = lax.dot_general(qh, kmb_ref[...], NT_DIMS, preferred_element_type=F32)
        beats = ((sa > sb) | ((sa == sb) & (m_of < n_of))) & pair_ok
        rank = jnp.dot(jnp.where(beats, 1.0, 0.0).astype(BF16), group, preferred_element_type=F32)
        selected = jnp.where((rank < MOBA_TOPK) & (pl_lane < i), 1.0, 0.0).astype(BF16)

        def logits(n):
            off = pl.multiple_of(n * t, t)
            z = lax.dot_general(qh, k_ref[pl.ds(off, t), :], NT_DIMS, preferred_element_type=F32)
            dist = row_minus_col + ((i - n) * t).astype(F32)
            return z - slope * dist, off

        z, off = logits(i)
        z = jnp.where(causal, z, NEG_INF)
        m_run = jnp.max(z, axis=-1, keepdims=True)
        e = jnp.exp(z - m_run)
        l_run = jnp.sum(e, axis=-1, keepdims=True)
        acc = jnp.dot(e.astype(BF16), v_ref[pl.ds(off, t), :], preferred_element_type=F32)

        def past_block(n, state):
            m_run, l_run, acc = state
            z, off = logits(n)
            pick = jnp.where(pick_row == n, 1.0, 0.0).astype(BF16)
            chosen = jnp.dot(selected, pick, preferred_element_type=F32)
            z = jnp.where(chosen > 0.5, z, NEG_INF)
            m_new = jnp.maximum(m_run, jnp.max(z, axis=-1, keepdims=True))
            alpha = jnp.exp(m_run - m_new)
            e = jnp.exp(z - m_new)
            l_new = alpha * l_run + jnp.sum(e, axis=-1, keepdims=True)
            acc = alpha * acc + jnp.dot(e.astype(BF16), v_ref[pl.ds(off, t), :], preferred_element_type=F32)
            return m_new, l_new, acc

        m_run, l_run, acc = lax.fori_loop(0, i, past_block, (m_run, l_run, acc))
        out = jnp.where(head_lanes, acc / l_run, out)
    o_ref[...] = out.astype(BF16)


def _moba_prompt(q, kb, vb, kmean, batch, seq):
    t = MOBA_BLOCK
    nb = seq // t
    assert seq % t == 0 and nb * nb <= LANES
    km = kmean.reshape(batch, nb, WIDTH).astype(BF16)
    pad = jnp.zeros((batch, LANES - nb * nb, WIDTH), BF16)
    kma = jnp.concatenate([jnp.repeat(km, nb, axis=1), pad], axis=1).reshape(batch * LANES, WIDTH)
    kmb = jnp.concatenate([jnp.tile(km, (1, nb, 1)), pad], axis=1).reshape(batch * LANES, WIDTH)
    return pl.pallas_call(
        functools.partial(_moba_prompt_kernel, nb=nb),
        grid_spec=pltpu.PrefetchScalarGridSpec(
            num_scalar_prefetch=1,
            grid=(batch, N_PAIRS, nb),
            in_specs=[
                pl.BlockSpec((t, PAIR), lambda b, p, i, s: (b * nb + i, p)),
                pl.BlockSpec((seq, PAIR), lambda b, p, i, s: (b, p)),
                pl.BlockSpec((seq, PAIR), lambda b, p, i, s: (b, p)),
                pl.BlockSpec((LANES, PAIR), lambda b, p, i, s: (b, p)),
                pl.BlockSpec((LANES, PAIR), lambda b, p, i, s: (b, p)),
            ],
            out_specs=pl.BlockSpec((t, PAIR), lambda b, p, i, s: (b * nb + i, p)),
        ),
        out_shape=jax.ShapeDtypeStruct((batch * seq, WIDTH), BF16),
        compiler_params=_params("parallel", "parallel", "parallel"),
        name="moba_prompt",
    )(_alibi_slopes(), q, kb, vb, kma, kmb)


CHUNK_PAGES = 16
CHUNK_KEYS = CHUNK_PAGES * PAGE_SIZE
SUB = 256


def _page_copies(pt_ref, pool_ref, buf_ref, sem_ref, layer, step, slot, n_chunks, descending):
    b = step // n_chunks
    c = step % n_chunks
    first_page = ((n_chunks - 1 - c) if descending else c) * CHUNK_PAGES
    return [
        pltpu.make_async_copy(pool_ref.at[layer, pt_ref[b, first_page + j]], buf_ref.at[slot, j], sem_ref.at[slot])
        for j in range(CHUNK_PAGES)
    ]


def _stream_chunk(pt_ref, pools_bufs_sems, layer, n_chunks, descending):
    step = pl.program_id(0) * n_chunks + pl.program_id(1)
    n_steps = pl.num_programs(0) * n_chunks
    slot = step % 2

    def start(s, sl):
        for pool_ref, buf_ref, sem_ref in pools_bufs_sems:
            for cp in _page_copies(pt_ref, pool_ref, buf_ref, sem_ref, layer, s, sl, n_chunks, descending):
                cp.start()

    @pl.when(step == 0)
    def _():
        start(step, slot)

    @pl.when(step + 1 < n_steps)
    def _():
        start(step + 1, 1 - slot)

    for pool_ref, buf_ref, sem_ref in pools_bufs_sems:
        for cp in _page_copies(pt_ref, pool_ref, buf_ref, sem_ref, layer, step, slot, n_chunks, descending):
            cp.wait()
    return slot


def _head_rows(x):
    rows = lax.broadcasted_iota(jnp.int32, (N_HEADS, WIDTH), 0)
    lanes = lax.broadcasted_iota(jnp.int32, (N_HEADS, WIDTH), 1)
    xb = jnp.broadcast_to(x.astype(F32), (N_HEADS, WIDTH))
    return jnp.where(lanes // HEAD_DIM == rows, xb, 0.0)


def _sb_sample_kernel(pt_ref, q_ref, kn_ref, vn_ref, uo_ref, kpool_ref, vpool_ref, o_ref,
                      kbuf, vbuf, ksem, vsem, carry_ref, acc_ref, *, layer, n_chunks):
    c = pl.program_id(1)
    slot = _stream_chunk(pt_ref, [(kpool_ref, kbuf, ksem), (vpool_ref, vbuf, vsem)], layer, n_chunks, True)
    qb = _head_rows(q_ref[...]).astype(BF16)
    qx = qb.astype(F32)

    @pl.when(c == 0)
    def _():
        t_new = lax.broadcasted_iota(jnp.int32, (1, 1), 0)
        allowed = t_new < t_new
        k_new = kn_ref[...].astype(BF16).astype(F32)
        z = jnp.sum(qx * k_new, axis=-1, keepdims=True)
        log_beta, log_keep = _log_sigmoids(z)
        a = jnp.where(allowed, jnp.exp(log_beta), 0.0)
        carry_ref[...] = jnp.broadcast_to(jnp.where(allowed, log_keep, 0.0), carry_ref.shape)
        acc_ref[...] = a * _head_rows(vn_ref[...].astype(BF16).astype(F32))

    z = jnp.concatenate([jnp.dot(qb, kbuf[slot, j].astype(BF16), preferred_element_type=F32)
                         for j in range(CHUNK_PAGES)], axis=1)
    log_beta, log_keep = _log_sigmoids(z)
    uo = uo_ref[...]
    n_sub = CHUNK_KEYS // SUB
    sums = [_suffix_sums(log_keep[:, s * SUB:(s + 1) * SUB], uo) for s in range(n_sub)]
    carry = carry_ref[...]
    weights = [None] * n_sub
    for s in reversed(range(n_sub)):
        later, total = sums[s]
        weights[s] = jnp.exp(log_beta[:, s * SUB:(s + 1) * SUB] + later + carry)
        carry = carry + total
    carry_ref[...] = carry
    acc = acc_ref[...]
    pages_per_sub = SUB // PAGE_SIZE
    for j in range(CHUNK_PAGES):
        w = weights[j // pages_per_sub]
        off = (j % pages_per_sub) * PAGE_SIZE
        acc = acc + lax.dot_general(w[:, off:off + PAGE_SIZE].astype(BF16), vbuf[slot, j].astype(BF16), NT_DIMS,
                                    preferred_element_type=F32)
    acc_ref[...] = acc

    @pl.when(c == n_chunks - 1)
    def _():
        rows = lax.broadcasted_iota(jnp.int32, (N_HEADS, WIDTH), 0)
        lanes = lax.broadcasted_iota(jnp.int32, (N_HEADS, WIDTH), 1)
        own = jnp.where(lanes // HEAD_DIM == rows, acc_ref[...], 0.0)
        o_ref[...] = jnp.sum(own, axis=0, keepdims=True)


def _pool_view(pool):
    depth, n_pool, page, heads, dh = pool.shape
    assert page == PAGE_SIZE and heads * dh == WIDTH
    return pool.transpose(0, 1, 3, 4, 2).reshape(depth, n_pool, WIDTH, PAGE_SIZE)


def _sb_sample(q, k_new, v_new, pool_k, pool_v, page_table, layer):
    db, n_pages = page_table.shape
    assert n_pages % CHUNK_PAGES == 0
    n_chunks = n_pages // CHUNK_PAGES
    row3 = pl.BlockSpec((None, 1, WIDTH), lambda b, c, pt: (b, 0, 0))
    return pl.pallas_call(
        functools.partial(_sb_sample_kernel, layer=layer, n_chunks=n_chunks),
        grid_spec=pltpu.PrefetchScalarGridSpec(
            num_scalar_prefetch=1,
            grid=(db, n_chunks),
            in_specs=[row3, row3, row3,
                      _resident((SUB, 2 * SUB), lambda b, c, pt: (0, 0)),
                      pl.BlockSpec(memory_space=pl.ANY),
                      pl.BlockSpec(memory_space=pl.ANY)],
            out_specs=row3,
            scratch_shapes=[
                pltpu.VMEM((2, CHUNK_PAGES, WIDTH, PAGE_SIZE), F32),
                pltpu.VMEM((2, CHUNK_PAGES, WIDTH, PAGE_SIZE), F32),
                pltpu.SemaphoreType.DMA((2,)),
                pltpu.SemaphoreType.DMA((2,)),
                pltpu.VMEM((N_HEADS, SUB), F32),
                pltpu.VMEM((N_HEADS, WIDTH), F32),
            ],
        ),
        out_shape=jax.ShapeDtypeStruct((db, 1, WIDTH), F32),
        compiler_params=_params("arbitrary", "arbitrary"),
        name="sb_sample",
    )(page_table, q.reshape(db, 1, WIDTH), k_new.reshape(db, 1, WIDTH), v_new.reshape(db, 1, WIDTH),
      _suffix_matrix(SUB), _pool_view(pool_k), _pool_view(pool_v))


def _moba_select_kernel(pt_ref, q_ref, kpool_ref, sel_ref, kbuf, ksem, kmean_ref, *, layer, n_chunks, n_sel):
    c = pl.program_id(1)
    slot = _stream_chunk(pt_ref, [(kpool_ref, kbuf, ksem)], layer, n_chunks, False)
    pages_per_block = MOBA_BLOCK // PAGE_SIZE
    blocks_per_chunk = CHUNK_PAGES // pages_per_block
    n_blocks = kmean_ref.shape[1]

    @pl.when(c == 0)
    def _():
        kmean_ref[...] = jnp.zeros_like(kmean_ref)

    block_lane = lax.broadcasted_iota(jnp.int32, kmean_ref.shape, 1)
    means = kmean_ref[...]
    for j in range(blocks_per_chunk):
        total = kbuf[slot, j * pages_per_block]
        for t in range(1, pages_per_block):
            total = total + kbuf[slot, j * pages_per_block + t]
        column = jnp.sum(total, axis=-1, keepdims=True) * (1.0 / MOBA_BLOCK)
        means = jnp.where(block_lane == c * blocks_per_chunk + j, column, means)
    kmean_ref[...] = means

    @pl.when(c == n_chunks - 1)
    def _():
        sc = _dot(_head_rows(q_ref[...]), means)
        lane = lax.broadcasted_iota(jnp.int32, sc.shape, 1).astype(F32)
        out_lane = lax.broadcasted_iota(jnp.int32, (N_HEADS, LANES), 1)
        picks = jnp.zeros((N_HEADS, LANES), jnp.int32)
        for r in range(n_sel):
            best = jnp.max(sc, axis=-1, keepdims=True)
            idx = jnp.min(jnp.where(sc == best, lane, float(n_blocks)), axis=-1, keepdims=True)
            picks = jnp.where(out_lane == r, idx.astype(jnp.int32), picks)
            sc = jnp.where(lane == idx, NEG_INF, sc)
        sel_ref[...] = picks


def _moba_select(q, pool_k, page_table, layer, n_sel):
    db, n_pages = page_table.shape
    n_chunks = n_pages // CHUNK_PAGES
    n_blocks = n_pages * PAGE_SIZE // MOBA_BLOCK
    return pl.pallas_call(
        functools.partial(_moba_select_kernel, layer=layer, n_chunks=n_chunks, n_sel=n_sel),
        grid_spec=pltpu.PrefetchScalarGridSpec(
            num_scalar_prefetch=1,
            grid=(db, n_chunks),
            in_specs=[pl.BlockSpec((None, 1, WIDTH), lambda b, c, pt: (b, 0, 0)),
                      pl.BlockSpec(memory_space=pl.ANY)],
            out_specs=pl.BlockSpec((None, N_HEADS, LANES), lambda b, c, pt: (b, 0, 0)),
            scratch_shapes=[
                pltpu.VMEM((2, CHUNK_PAGES, WIDTH, PAGE_SIZE), F32),
                pltpu.SemaphoreType.DMA((2,)),
                pltpu.VMEM((WIDTH, n_blocks), F32),
            ],
        ),
        out_shape=jax.ShapeDtypeStruct((db, N_HEADS, LANES), jnp.int32),
        compiler_params=_params("arbitrary", "arbitrary"),
        name="moba_select",
    )(page_table, q.reshape(db, 1, WIDTH), _pool_view(pool_k))


def _moba_sample_kernel(pt_ref, sel_ref, slopes_ref, q_ref, kn_ref, vn_ref, kpool_ref, vpool_ref, o_ref,
                        kbuf, vbuf, sem, *, layer, n_sel, past):
    b = pl.program_id(0)
    pages_per_block = MOBA_BLOCK // PAGE_SIZE
    n_keys = n_sel * MOBA_BLOCK

    def copies():
        out = []
        for h in range(N_HEADS):
            dims = pl.ds(h * HEAD_DIM, HEAD_DIM)
            for r in range(n_sel):
                blk = sel_ref[b, h * n_sel + r]
                for j in range(pages_per_block):
                    page = pt_ref[b, blk * pages_per_block + j]
                    keys = pl.ds((r * pages_per_block + j) * PAGE_SIZE, PAGE_SIZE)
                    out.append(pltpu.make_async_copy(kpool_ref.at[layer, page, dims], kbuf.at[h, :, keys], sem.at[0]))
                    out.append(pltpu.make_async_copy(vpool_ref.at[layer, page, dims], vbuf.at[h, :, keys], sem.at[1]))
        return out

    for cp in copies():
        cp.start()
    for cp in copies():
        cp.wait()

    col = lax.broadcasted_iota(jnp.int32, (1, n_keys), 1)
    rank_of = col // MOBA_BLOCK
    head_row = lax.broadcasted_iota(jnp.int32, (N_HEADS, HEAD_DIM), 0)
    q_all = q_ref[...].astype(BF16).astype(F32)
    k_new = kn_ref[...].astype(BF16).astype(F32)
    v_new = vn_ref[...].astype(BF16).astype(F32)
    out = jnp.zeros((N_HEADS, HEAD_DIM), F32)
    for h in range(N_HEADS):
        qh = q_all[h:h + 1]
        q_rows = jnp.broadcast_to(qh, (N_HEADS, HEAD_DIM)).astype(BF16)
        slope = slopes_ref[h]
        z = jnp.dot(q_rows, kbuf[h].astype(BF16), preferred_element_type=F32)[0:1]
        blk = jnp.zeros((1, n_keys), jnp.int32)
        for r in range(n_sel):
            blk = jnp.where(rank_of == r, sel_ref[b, h * n_sel + r], blk)
        pos = blk * MOBA_BLOCK + col % MOBA_BLOCK
        z = z - slope * (past - pos).astype(F32)
        z_own = jnp.sum(qh * k_new[h:h + 1], axis=-1, keepdims=True)
        m = jnp.maximum(jnp.max(z, axis=-1, keepdims=True), z_own)
        e = jnp.exp(z - m)
        e_own = jnp.exp(z_own - m)
        denom = jnp.sum(e, axis=-1, keepdims=True) + e_own
        pv = lax.dot_general(jnp.broadcast_to(e, (N_HEADS, n_keys)).astype(BF16), vbuf[h].astype(BF16), NT_DIMS,
                             preferred_element_type=F32)[0:1]
        o_h = (pv + e_own * v_new[h:h + 1]) / denom
        out = jnp.where(head_row == h, jnp.broadcast_to(o_h, out.shape), out)
    o_ref[...] = out


def _moba_sample(q, k_new, v_new, pool_k, pool_v, page_table, sel, layer, n_sel):
    db, n_pages = page_table.shape
    past = n_pages * PAGE_SIZE
    assert past % MOBA_BLOCK == 0, "own block must hold no cached rows"
    heads = pl.BlockSpec((None, N_HEADS, HEAD_DIM), lambda b, pt, s, sl: (b, 0, 0))
    as_heads = lambda x: x.reshape(db, N_HEADS, HEAD_DIM)
    return pl.pallas_call(
        functools.partial(_moba_sample_kernel, layer=layer, n_sel=n_sel, past=past),
        grid_spec=pltpu.PrefetchScalarGridSpec(
            num_scalar_prefetch=3,
            grid=(db,),
            in_specs=[heads, heads, heads,
                      pl.BlockSpec(memory_space=pl.ANY),
                      pl.BlockSpec(memory_space=pl.ANY)],
            out_specs=heads,
            scratch_shapes=[
                pltpu.VMEM((N_HEADS, HEAD_DIM, n_sel * MOBA_BLOCK), F32),
                pltpu.VMEM((N_HEADS, HEAD_DIM, n_sel * MOBA_BLOCK), F32),
                pltpu.SemaphoreType.DMA((2,)),
            ],
        ),
        out_shape=jax.ShapeDtypeStruct((db, N_HEADS, HEAD_DIM), F32),
        compiler_params=_params("arbitrary"),
        name="moba_sample",
    )(page_table, sel, _alibi_slopes(), as_heads(q), as_heads(k_new), as_heads(v_new),
      _pool_view(pool_k), _pool_view(pool_v))


def kernel(x_prompt, x_sample, cache_sb_k, cache_sb_v, cache_moba_k, cache_moba_v, page_table, ffn1_norm_pre, ffn1_norm_post, ffn1_w_up, ffn1_w_down, mix_norm_pre, mix_norm_post, w_in, b_gate, w_branch_sb, w_branch_moba, w_out, ffn2_norm_pre, ffn2_norm_post, ffn2_w_up, ffn2_w_down):
    batch, seq, d = x_prompt.shape
    db, dec_seq, _ = x_sample.shape
    assert dec_seq == 1, "the sample kernels take one new token per sequence"
    depth = w_in.shape[0]
    n_pages = page_table.shape[1]
    n_full = n_pages * PAGE_SIZE // MOBA_BLOCK
    n_sel = min(MOBA_TOPK, n_full)
    assert n_sel >= 1

    hp = x_prompt.reshape(batch * seq, d)
    hs = x_sample.reshape(db * dec_seq, d)
    carried = [jnp.zeros((depth, batch, WIDTH, seq), F32) for _ in range(4)]
    rows_s = [[], [], [], []]
    for l in range(depth):
        w1u, w1d = ffn1_w_up[l].astype(BF16), ffn1_w_down[l].astype(BF16)
        w2u, w2d = ffn2_w_up[l].astype(BF16), ffn2_w_down[l].astype(BF16)
        wi = w_in[l].astype(BF16)
        wbs, wbm, wo = w_branch_sb[l].astype(BF16), w_branch_moba[l].astype(BF16), w_out[l].astype(BF16)

        hp = _ffn(hp, ffn1_norm_pre[l], ffn1_norm_post[l], w1u, w1d)
        qa, kab, vab, qm, kmb, vmb, gates, kmean, *carried = _in_proj(
            hp, mix_norm_pre[l], wi, layer=l, depth=depth, batch=batch, carried=tuple(carried))
        o_sb = _sb_prompt(qa, kab, vab, batch, seq)
        o_m = _moba_prompt(qm, kmb, vmb, kmean, batch, seq)
        hp = _merge(hp, o_sb, o_m, gates, b_gate[l], wbs, wbm, wo, mix_norm_post[l])
        hp = _ffn(hp, ffn2_norm_pre[l], ffn2_norm_post[l], w2u, w2d)

        hs = _ffn(hs, ffn1_norm_pre[l], ffn1_norm_post[l], ffn1_w_up[l], ffn1_w_down[l])
        qa, ka, va, qm, km, vm, gates = _in_proj(hs, mix_norm_pre[l], w_in[l])
        o_sb = _sb_sample(qa, ka, va, cache_sb_k, cache_sb_v, page_table, l)
        sel = _moba_select(qm, cache_moba_k, page_table, l, n_sel)[:, :, :n_sel].reshape(db, N_HEADS * n_sel)
        o_m = _moba_sample(qm, km, vm, cache_moba_k, cache_moba_v, page_table, sel, l, n_sel)
        hs = _merge(hs, o_sb.reshape(db, WIDTH), o_m.reshape(db, WIDTH), gates, b_gate[l],
                    w_branch_sb[l], w_branch_moba[l], w_out[l], mix_norm_post[l])
        hs = _ffn(hs, ffn2_norm_pre[l], ffn2_norm_post[l], ffn2_w_up[l], ffn2_w_down[l])
        for acc, r in zip(rows_s, (ka, va, km, vm)):
            acc.append(r.reshape(db, dec_seq, N_HEADS, HEAD_DIM))

    rows_p = [t.reshape(depth, batch, N_HEADS, HEAD_DIM, seq).transpose(0, 1, 4, 2, 3) for t in carried]
    return (hp.reshape(batch, seq, d), hs.reshape(db, dec_seq, d),
            *rows_p, *(jnp.stack(r) for r in rows_s))
```

```python
import functools

import numpy as np
import jax
import jax.numpy as jnp
from jax import lax
from jax.experimental import pallas as pl
from jax.experimental.pallas import tpu as pltpu

HEAD_DIM = 64
N_HEADS = 8
WIDTH = N_HEADS * HEAD_DIM
PAIR = 2 * HEAD_DIM
N_PAIRS = N_HEADS // 2
MOBA_BLOCK = 256
MOBA_TOPK = 3
PAGE_SIZE = 128
NORM_EPS = 1e-6
LANES = 128
VMEM_LIMIT = 56 * 1024 * 1024

F32 = jnp.float32
BF16 = jnp.bfloat16
NEG_INF = float("-inf")
NT_DIMS = (((1,), (1,)), ((), ()))


def _params(*semantics):
    return pltpu.CompilerParams(dimension_semantics=semantics, vmem_limit_bytes=VMEM_LIMIT)


def _rms(x, g):
    return x * lax.rsqrt(jnp.mean(x * x, axis=-1, keepdims=True) + NORM_EPS) * g


def _resident(shape, index_map):
    return pl.BlockSpec(shape, index_map, pipeline_mode=pl.Buffered(1))


FFN_CHUNK = 256


def _dot(a, b, dims=None):
    precision = lax.Precision.HIGHEST if a.dtype == F32 else None
    if dims is None:
        return jnp.dot(a, b, preferred_element_type=F32, precision=precision)
    return lax.dot_general(a, b, dims, preferred_element_type=F32, precision=precision)


def _ffn_kernel(x_ref, gpre_ref, gpost_ref, wa_ref, wb_ref, wd_ref, o_ref, acc_ref, u_ref):
    j = pl.program_id(1)

    @pl.when(j == 0)
    def _():
        u_ref[...] = _rms(x_ref[...], gpre_ref[...]).astype(u_ref.dtype)
        acc_ref[...] = jnp.zeros_like(acc_ref)

    u = u_ref[...]
    for c in range(wd_ref.shape[0] // FFN_CHUNK):
        cols = slice(c * FFN_CHUNK, (c + 1) * FFN_CHUNK)
        a = _dot(u, wa_ref[:, cols])
        b = _dot(u, wb_ref[:, cols])
        mid = (a * jax.nn.sigmoid(a) * b).astype(u.dtype)
        acc_ref[...] += _dot(mid, wd_ref[cols, :])

    @pl.when(j == pl.num_programs(1) - 1)
    def _():
        o_ref[...] = x_ref[...] + 0.5 * _rms(acc_ref[...], gpost_ref[...])


def _ffn(x, g_pre, g_post, w_up, w_down):
    m, d = x.shape
    f = w_down.shape[0]
    assert f % FFN_CHUNK == 0 and w_up.shape == (d, 2 * f)
    tm = min(512, m)
    assert m % tm == 0
    if w_up.dtype == BF16:
        tf, spec = f, _resident
    else:
        tf, spec = FFN_CHUNK, pl.BlockSpec
    n_slabs = f // tf
    return pl.pallas_call(
        _ffn_kernel,
        grid=(m // tm, n_slabs),
        in_specs=[
            pl.BlockSpec((tm, d), lambda i, j: (i, 0)),
            _resident((1, d), lambda i, j: (0, 0)),
            _resident((1, d), lambda i, j: (0, 0)),
            spec((d, tf), lambda i, j: (0, j)),
            spec((d, tf), lambda i, j: (0, n_slabs + j)),
            spec((tf, d), lambda i, j: (j, 0)),
        ],
        out_specs=pl.BlockSpec((tm, d), lambda i, j: (i, 0)),
        out_shape=jax.ShapeDtypeStruct((m, d), F32),
        scratch_shapes=[pltpu.VMEM((tm, d), F32), pltpu.VMEM((tm, d), w_up.dtype)],
        compiler_params=_params("parallel", "arbitrary"),
        name="ffn",
    )(x, g_pre.reshape(1, d), g_post.reshape(1, d), w_up, w_up, w_down)


def _in_kernel(*refs, prompt, n_carried):
    h_ref, g_ref, w_ref = refs[:3]
    outs = refs[3 + n_carried:]
    qa_ref, ka_ref, va_ref, qm_ref, km_ref, vm_ref, gate_ref = outs[:7]
    if prompt:
        kmean_ref, kat_ref, vat_ref, kmt_ref, vmt_ref = outs[7:]
    else:
        kat_ref = vat_ref = kmt_ref = vmt_ref = None
    u = _rms(h_ref[...], g_ref[...]).astype(w_ref.dtype)
    scale = HEAD_DIM ** -0.5

    def proj(c):
        return _dot(u, w_ref[:, c * WIDTH:(c + 1) * WIDTH])

    def emit(c, row_ref, t_ref):
        r = proj(c)
        row_ref[...] = r.astype(row_ref.dtype)
        if prompt:
            t_ref[...] = r.T
        return r

    qa_ref[...] = (proj(0) * scale).astype(qa_ref.dtype)
    emit(1, ka_ref, kat_ref)
    emit(2, va_ref, vat_ref)
    qm_ref[...] = (proj(3) * scale).astype(qm_ref.dtype)
    r = emit(4, km_ref, kmt_ref)
    if prompt:
        for j in range(r.shape[0] // MOBA_BLOCK):
            blk = r[j * MOBA_BLOCK:(j + 1) * MOBA_BLOCK]
            kmean_ref[j] = jnp.sum(blk, axis=0, keepdims=True) * (1.0 / MOBA_BLOCK)
    emit(5, vm_ref, vmt_ref)
    for c in range(gate_ref.shape[1] // WIDTH):
        gate_ref[:, c * WIDTH:(c + 1) * WIDTH] = proj(6 + c)


def _in_proj(h, g, w_in, *, layer=0, depth=1, batch=None, carried=()):
    m, d = h.shape
    n_cols = w_in.shape[1]
    assert n_cols == 6 * WIDTH + 2 * d
    prompt = batch is not None
    tm = min(512, m)
    assert m % tm == 0
    row = lambda i: (i, 0)
    wide = pl.BlockSpec((tm, WIDTH), row)
    kv_dtype = BF16 if prompt else F32
    out_specs = [wide] * 6 + [pl.BlockSpec((tm, 2 * d), row)]
    out_shape = [jax.ShapeDtypeStruct((m, WIDTH), kv_dtype)] * 6
    out_shape.append(jax.ShapeDtypeStruct((m, 2 * d), F32))
    aliases = {}
    if prompt:
        seq = m // batch
        assert seq % tm == 0 and tm % MOBA_BLOCK == 0
        tiles = seq // tm
        out_specs.append(pl.BlockSpec((tm // MOBA_BLOCK, 1, WIDTH), lambda i: (i, 0, 0)))
        out_shape.append(jax.ShapeDtypeStruct((m // MOBA_BLOCK, 1, WIDTH), F32))
        out_specs += [pl.BlockSpec((None, None, WIDTH, tm), lambda i: (layer, i // tiles, 0, i % tiles))] * 4
        out_shape += [jax.ShapeDtypeStruct((depth, batch, WIDTH, seq), F32)] * 4
        assert len(carried) == 4
        aliases = {3 + j: 8 + j for j in range(4)}
    return pl.pallas_call(
        functools.partial(_in_kernel, prompt=prompt, n_carried=len(carried)),
        grid=(m // tm,),
        in_specs=[
            pl.BlockSpec((tm, d), row),
            _resident((1, d), lambda i: (0, 0)),
            _resident((d, n_cols), lambda i: (0, 0)),
        ] + [pl.BlockSpec(memory_space=pl.ANY)] * len(carried),
        out_specs=out_specs,
        out_shape=out_shape,
        input_output_aliases=aliases,
        compiler_params=_params("parallel"),
        name="in_proj",
    )(h, g.reshape(1, d), w_in, *carried)


def _merge_kernel(h_ref, osb_ref, om_ref, gate_ref, bg_ref, wbs_ref, wbm_ref, wo_ref, gpost_ref, o_ref):
    d = h_ref.shape[1]
    a = _dot(osb_ref[...], wbs_ref[...])
    b = _dot(om_ref[...], wbm_ref[...])
    merged = (jax.nn.sigmoid(gate_ref[:, :d] + bg_ref[:, :d]) * a
              + jax.nn.sigmoid(gate_ref[:, d:] + bg_ref[:, d:]) * b)
    y = _dot(merged.astype(wo_ref.dtype), wo_ref[...])
    o_ref[...] = h_ref[...] + _rms(y, gpost_ref[...])


def _merge(h, o_sb, o_m, gates, b_gate, w_bs, w_bm, w_out, g_post):
    m, d = h.shape
    tm = min(512, m)
    row = lambda i: (i, 0)
    const = lambda i: (0, 0)
    return pl.pallas_call(
        _merge_kernel,
        grid=(m // tm,),
        in_specs=[
            pl.BlockSpec((tm, d), row),
            pl.BlockSpec((tm, WIDTH), row),
            pl.BlockSpec((tm, WIDTH), row),
            pl.BlockSpec((tm, 2 * d), row),
            _resident((1, 2 * d), const),
            _resident((WIDTH, d), const),
            _resident((WIDTH, d), const),
            _resident((d, d), const),
            _resident((1, d), const),
        ],
        out_specs=pl.BlockSpec((tm, d), row),
        out_shape=jax.ShapeDtypeStruct((m, d), F32),
        compiler_params=_params("parallel"),
        name="merge",
    )(h, o_sb, o_m, gates, b_gate.reshape(1, 2 * d), w_bs, w_bm, w_out, g_post.reshape(1, d))


def _suffix_matrix(n):
    j = np.arange(n)[:, None]
    s = np.arange(n)[None, :]
    u1 = np.concatenate([(j > s), np.ones((n, n), bool)], axis=1)
    return jnp.asarray(np.concatenate([u1, u1], axis=0), dtype=BF16)


def _log_sigmoids(z):
    log_beta = jnp.minimum(z, 0.0) - jnp.log(1.0 + jnp.exp(-jnp.abs(z)))
    return log_beta, log_beta - z


def _suffix_sums(x, uo):
    hi = x.astype(BF16)
    lo = (x - hi.astype(F32)).astype(BF16)
    t = jnp.dot(jnp.concatenate([hi, lo], axis=1), uo, preferred_element_type=F32)
    n = x.shape[1]
    return t[:, :n], t[:, n:]


SB_Q_TILE = 256
SB_K_TILE = 128


def _sb_prompt_kernel(q_ref, k_ref, v_ref, uo_ref, o_ref, carry_ref, acc_ref):
    i = pl.program_id(2)
    tq, tk = SB_Q_TILE, SB_K_TILE
    per = tq // tk
    q = q_ref[...]
    uo = uo_ref[...]
    head0 = (lax.broadcasted_iota(jnp.int32, (1, PAIR), 1) // HEAD_DIM) == 0
    row = lax.broadcasted_iota(jnp.int32, (tq, tk), 0)
    col = lax.broadcasted_iota(jnp.int32, (tq, tk), 1)
    carry_ref[...] = jnp.zeros_like(carry_ref)
    acc_ref[...] = jnp.zeros_like(acc_ref)

    def by_head(x):
        zero = jnp.zeros_like(x)
        return jnp.concatenate([jnp.where(head0, x, zero), jnp.where(head0, zero, x)], axis=0)

    def sweep(key_tiles, masks):
        pending = []
        for jt, mask in zip(key_tiles, masks):
            off = pl.multiple_of(jt * tk, tk)
            z2 = lax.dot_general(q, by_head(k_ref[pl.ds(off, tk), :]), NT_DIMS, preferred_element_type=F32)
            partial, total = [], []
            for hh in range(2):
                log_beta, log_keep = _log_sigmoids(z2[:, hh * tk:(hh + 1) * tk])
                if mask is not None:
                    log_keep = jnp.where(mask, log_keep, 0.0)
                later, tot = _suffix_sums(log_keep, uo)
                partial.append(log_beta + later)
                total.append(tot)
            pending.append((off, mask, partial, total))
        for off, mask, partial, total in pending:
            weights = []
            for hh in range(2):
                carry = carry_ref[hh]
                a = jnp.exp(partial[hh] + carry)
                if mask is not None:
                    a = jnp.where(mask, a, 0.0)
                weights.append(a.astype(BF16))
                carry_ref[hh] = carry + total[hh]
            acc_ref[...] += jnp.dot(jnp.concatenate(weights, axis=1), by_head(v_ref[pl.ds(off, tk), :]),
                                    preferred_element_type=F32)

    inner = list(reversed(range(per)))
    sweep([i * per + d for d in inner], [col + d * tk < row for d in inner])

    def past(s, _):
        top = (i - s) * per - 1
        sweep([top - d for d in range(per)], [None] * per)
        return 0

    lax.fori_loop(0, i, past, 0)
    o_ref[...] = acc_ref[...].astype(BF16)


def _sb_prompt(q, kb, vb, batch, seq):
    tq, tk = SB_Q_TILE, SB_K_TILE
    nq = seq // tq
    assert seq % tq == 0
    return pl.pallas_call(
        _sb_prompt_kernel,
        grid=(batch, N_PAIRS, nq),
        in_specs=[
            pl.BlockSpec((tq, PAIR), lambda b, p, i: (b * nq + i, p)),
            pl.BlockSpec((seq, PAIR), lambda b, p, i: (b, p)),
            pl.BlockSpec((seq, PAIR), lambda b, p, i: (b, p)),
            _resident((2 * tk, 2 * tk), lambda b, p, i: (0, 0)),
        ],
        out_specs=pl.BlockSpec((tq, PAIR), lambda b, p, i: (b * nq + i, p)),
        out_shape=jax.ShapeDtypeStruct((batch * seq, WIDTH), BF16),
        scratch_shapes=[pltpu.VMEM((2, tq, tk), F32), pltpu.VMEM((tq, PAIR), F32)],
        compiler_params=_params("parallel", "parallel", "parallel"),
        name="sb_prompt",
    )(q, kb, vb, _suffix_matrix(tk))


def _alibi_slopes():
    return jnp.asarray(2.0 ** (-8.0 * np.arange(1, N_HEADS + 1) / N_HEADS), dtype=F32)


MASKED = -1e30


def _moba_prompt_kernel(slopes_ref, q_ref, k_ref, v_ref, kma_ref, kmb_ref, o_ref,
                        s_ref, mx_ref, ls_ref, acc_ref, *, nb):
    p = pl.program_id(1)
    i = pl.program_id(2)
    t = MOBA_BLOCK
    q = q_ref[...]
    lane = lax.broadcasted_iota(jnp.int32, (1, PAIR), 1)
    row = lax.broadcasted_iota(jnp.int32, (t, t), 0)
    col = lax.broadcasted_iota(jnp.int32, (t, t), 1)
    row_minus_col = (row - col).astype(F32)
    causal = col <= row
    pl_lane = lax.broadcasted_iota(jnp.int32, (t, LANES), 1)
    m_of = pl_lane // nb
    n_of = pl_lane % nb
    pair_ok = (pl_lane < nb * nb) & (m_of < i)
    gl = lax.broadcasted_iota(jnp.int32, (LANES, LANES), 0)
    gn = lax.broadcasted_iota(jnp.int32, (LANES, LANES), 1)
    group = jnp.where((gl < nb * nb) & (gl % nb == gn), 1.0, 0.0).astype(BF16)
    pick_row = lax.broadcasted_iota(jnp.int32, (LANES, t), 0)
    qh, slope, unselected = [], [], []
    for hh in range(2):
        qh.append(jnp.where((lane // HEAD_DIM) == hh, q, jnp.zeros_like(q)))
        slope.append(slopes_ref[2 * p + hh])
        sa = lax.dot_general(qh[hh], kma_ref[...], NT_DIMS, preferred_element_type=F32)
        sb = lax.dot_general(qh[hh], kmb_ref[...], NT_DIMS, preferred_element_type=F32)
        beats = ((sa > sb) | ((sa == sb) & (m_of < n_of))) & pair_ok
        rank = jnp.dot(jnp.where(beats, 1.0, 0.0).astype(BF16), group, preferred_element_type=F32)
        unselected.append(jnp.where((rank < MOBA_TOPK) & (pl_lane < i), 0.0, 1.0).astype(BF16))

    def logits(hh, n):
        off = pl.multiple_of(n * t, t)
        z = lax.dot_general(qh[hh], k_ref[pl.ds(off, t), :], NT_DIMS, preferred_element_type=F32)
        return z - slope[hh] * (row_minus_col + ((i - n) * t).astype(F32))

    for hh in range(2):
        z = jnp.where(causal, logits(hh, i), MASKED)
        s_ref[hh, i] = z
        mx_ref[hh] = z

    def scan(n, _):
        pick = jnp.where(pick_row == n, MASKED, 0.0).astype(BF16)
        for hh in range(2):
            z = logits(hh, n) + jnp.dot(unselected[hh], pick, preferred_element_type=F32)
            s_ref[hh, n] = z
            mx_ref[hh] = jnp.maximum(mx_ref[hh], z)
        return 0

    lax.fori_loop(0, i, scan, 0)

    m = [jnp.max(mx_ref[hh], axis=-1, keepdims=True) for hh in range(2)]
    ls_ref[...] = jnp.zeros_like(ls_ref)
    acc_ref[...] = jnp.zeros_like(acc_ref)

    def accumulate(n, _):
        v = v_ref[pl.ds(pl.multiple_of(n * t, t), t), :]
        for hh in range(2):
            e = jnp.exp(s_ref[hh, n] - m[hh])
            ls_ref[hh] += e
            acc_ref[hh] += jnp.dot(e.astype(BF16), v, preferred_element_type=F32)
        return 0

    lax.fori_loop(0, i + 1, accumulate, 0)
    o = [acc_ref[hh] / jnp.sum(ls_ref[hh], axis=-1, keepdims=True) for hh in range(2)]
    o_ref[...] = jnp.where((lane // HEAD_DIM) == 0, o[0], o[1]).astype(BF16)


def _moba_prompt(q, kb, vb, kmean, batch, seq):
    t = MOBA_BLOCK
    nb = seq // t
    assert seq % t == 0 and nb * nb <= LANES
    km = kmean.reshape(batch, nb, WIDTH).astype(BF16)
    pad = jnp.zeros((batch, LANES - nb * nb, WIDTH), BF16)
    kma = jnp.concatenate([jnp.repeat(km, nb, axis=1), pad], axis=1).reshape(batch * LANES, WIDTH)
    kmb = jnp.concatenate([jnp.tile(km, (1, nb, 1)), pad], axis=1).reshape(batch * LANES, WIDTH)
    return pl.pallas_call(
        functools.partial(_moba_prompt_kernel, nb=nb),
        grid_spec=pltpu.PrefetchScalarGridSpec(
            num_scalar_prefetch=1,
            grid=(batch, N_PAIRS, nb),
            in_specs=[
                pl.BlockSpec((t, PAIR), lambda b, p, i, s: (b * nb + i, p)),
                pl.BlockSpec((seq, PAIR), lambda b, p, i, s: (b, p)),
                pl.BlockSpec((seq, PAIR), lambda b, p, i, s: (b, p)),
                pl.BlockSpec((LANES, PAIR), lambda b, p, i, s: (b, p)),
                pl.BlockSpec((LANES, PAIR), lambda b, p, i, s: (b, p)),
            ],
            out_specs=pl.BlockSpec((t, PAIR), lambda b, p, i, s: (b * nb + i, p)),
            scratch_shapes=[
                pltpu.VMEM((2, nb, t, t), F32),
                pltpu.VMEM((2, t, t), F32),
                pltpu.VMEM((2, t, t), F32),
                pltpu.VMEM((2, t, PAIR), F32),
            ],
        ),
        out_shape=jax.ShapeDtypeStruct((batch * seq, WIDTH), BF16),
        compiler_params=_params("parallel", "parallel", "parallel"),
        name="moba_prompt",
    )(_alibi_slopes(), q, kb, vb, kma, kmb)


CHUNK_PAGES = 16
CHUNK_KEYS = CHUNK_PAGES * PAGE_SIZE
SUB = 256


def _page_copies(pt_ref, pool_ref, buf_ref, sem_ref, layer, step, slot, n_chunks, descending):
    b = step // n_chunks
    c = step % n_chunks
    first_page = ((n_chunks - 1 - c) if descending else c) * CHUNK_PAGES
    return [
        pltpu.make_async_copy(pool_ref.at[layer, pt_ref[b, first_page + j]], buf_ref.at[slot, j], sem_ref.at[slot])
        for j in range(CHUNK_PAGES)
    ]


def _stream_chunk(pt_ref, pools_bufs_sems, layer, n_chunks, descending):
    step = pl.program_id(0) * n_chunks + pl.program_id(1)
    n_steps = pl.num_programs(0) * n_chunks
    slot = step % 2

    def start(s, sl):
        for pool_ref, buf_ref, sem_ref in pools_bufs_sems:
            for cp in _page_copies(pt_ref, pool_ref, buf_ref, sem_ref, layer, s, sl, n_chunks, descending):
                cp.start()

    @pl.when(step == 0)
    def _():
        start(step, slot)

    @pl.when(step + 1 < n_steps)
    def _():
        start(step + 1, 1 - slot)

    for pool_ref, buf_ref, sem_ref in pools_bufs_sems:
        for cp in _page_copies(pt_ref, pool_ref, buf_ref, sem_ref, layer, step, slot, n_chunks, descending):
            cp.wait()
    return slot


def _head_rows(x):
    rows = lax.broadcasted_iota(jnp.int32, (N_HEADS, WIDTH), 0)
    lanes = lax.broadcasted_iota(jnp.int32, (N_HEADS, WIDTH), 1)
    xb = jnp.broadcast_to(x.astype(F32), (N_HEADS, WIDTH))
    return jnp.where(lanes // HEAD_DIM == rows, xb, 0.0)


def _sb_sample_kernel(pt_ref, q_ref, kn_ref, vn_ref, uo_ref, kpool_ref, vpool_ref, o_ref,
                      kbuf, vbuf, ksem, vsem, carry_ref, acc_ref, *, layer, n_chunks):
    c = pl.program_id(1)
    slot = _stream_chunk(pt_ref, [(kpool_ref, kbuf, ksem), (vpool_ref, vbuf, vsem)], layer, n_chunks, True)
    qb = _head_rows(q_ref[...]).astype(BF16)
    qx = qb.astype(F32)

    @pl.when(c == 0)
    def _():
        t_new = lax.broadcasted_iota(jnp.int32, (1, 1), 0)
        allowed = t_new < t_new
        k_new = kn_ref[...].astype(BF16).astype(F32)
        z = jnp.sum(qx * k_new, axis=-1, keepdims=True)
        log_beta, log_keep = _log_sigmoids(z)
        a = jnp.where(allowed, jnp.exp(log_beta), 0.0)
        carry_ref[...] = jnp.broadcast_to(jnp.where(allowed, log_keep, 0.0), carry_ref.shape)
        acc_ref[...] = a * _head_rows(vn_ref[...].astype(BF16).astype(F32))

    z = jnp.concatenate([jnp.dot(qb, kbuf[slot, j].astype(BF16), preferred_element_type=F32)
                         for j in range(CHUNK_PAGES)], axis=1)
    log_beta, log_keep = _log_sigmoids(z)
    uo = uo_ref[...]
    n_sub = CHUNK_KEYS // SUB
    sums = [_suffix_sums(log_keep[:, s * SUB:(s + 1) * SUB], uo) for s in range(n_sub)]
    carry = carry_ref[...]
    weights = [None] * n_sub
    for s in reversed(range(n_sub)):
        later, total = sums[s]
        weights[s] = jnp.exp(log_beta[:, s * SUB:(s + 1) * SUB] + later + carry)
        carry = carry + total
    carry_ref[...] = carry
    acc = acc_ref[...]
    pages_per_sub = SUB // PAGE_SIZE
    for j in range(CHUNK_PAGES):
        w = weights[j // pages_per_sub]
        off = (j % pages_per_sub) * PAGE_SIZE
        acc = acc + lax.dot_general(w[:, off:off + PAGE_SIZE].astype(BF16), vbuf[slot, j].astype(BF16), NT_DIMS,
                                    preferred_element_type=F32)
    acc_ref[...] = acc

    @pl.when(c == n_chunks - 1)
    def _():
        rows = lax.broadcasted_iota(jnp.int32, (N_HEADS, WIDTH), 0)
        lanes = lax.broadcasted_iota(jnp.int32, (N_HEADS, WIDTH), 1)
        own = jnp.where(lanes // HEAD_DIM == rows, acc_ref[...], 0.0)
        o_ref[...] = jnp.sum(own, axis=0, keepdims=True)


def _pool_view(pool):
    depth, n_pool, page, heads, dh = pool.shape
    assert page == PAGE_SIZE and heads * dh == WIDTH
    return pool.transpose(0, 1, 3, 4, 2).reshape(depth, n_pool, WIDTH, PAGE_SIZE)


def _sb_sample(q, k_new, v_new, pool_k, pool_v, page_table, layer):
    db, n_pages = page_table.shape
    assert n_pages % CHUNK_PAGES == 0
    n_chunks = n_pages // CHUNK_PAGES
    row3 = pl.BlockSpec((None, 1, WIDTH), lambda b, c, pt: (b, 0, 0))
    return pl.pallas_call(
        functools.partial(_sb_sample_kernel, layer=layer, n_chunks=n_chunks),
        grid_spec=pltpu.PrefetchScalarGridSpec(
            num_scalar_prefetch=1,
            grid=(db, n_chunks),
            in_specs=[row3, row3, row3,
                      _resident((2 * SUB, 2 * SUB), lambda b, c, pt: (0, 0)),
                      pl.BlockSpec(memory_space=pl.ANY),
                      pl.BlockSpec(memory_space=pl.ANY)],
            out_specs=row3,
            scratch_shapes=[
                pltpu.VMEM((2, CHUNK_PAGES, WIDTH, PAGE_SIZE), F32),
                pltpu.VMEM((2, CHUNK_PAGES, WIDTH, PAGE_SIZE), F32),
                pltpu.SemaphoreType.DMA((2,)),
                pltpu.SemaphoreType.DMA((2,)),
                pltpu.VMEM((N_HEADS, SUB), F32),
                pltpu.VMEM((N_HEADS, WIDTH), F32),
            ],
        ),
        out_shape=jax.ShapeDtypeStruct((db, 1, WIDTH), F32),
        compiler_params=_params("arbitrary", "arbitrary"),
        name="sb_sample",
    )(page_table, q.reshape(db, 1, WIDTH), k_new.reshape(db, 1, WIDTH), v_new.reshape(db, 1, WIDTH),
      _suffix_matrix(SUB), _pool_view(pool_k), _pool_view(pool_v))


def _moba_select_kernel(pt_ref, q_ref, kpool_ref, sel_ref, kbuf, ksem, kmean_ref, *, layer, n_chunks, n_sel):
    c = pl.program_id(1)
    slot = _stream_chunk(pt_ref, [(kpool_ref, kbuf, ksem)], layer, n_chunks, False)
    pages_per_block = MOBA_BLOCK // PAGE_SIZE
    blocks_per_chunk = CHUNK_PAGES // pages_per_block
    n_blocks = kmean_ref.shape[1]

    @pl.when(c == 0)
    def _():
        kmean_ref[...] = jnp.zeros_like(kmean_ref)

    block_lane = lax.broadcasted_iota(jnp.int32, kmean_ref.shape, 1)
    means = kmean_ref[...]
    for j in range(blocks_per_chunk):
        total = kbuf[slot, j * pages_per_block]
        for t in range(1, pages_per_block):
            total = total + kbuf[slot, j * pages_per_block + t]
        column = jnp.sum(total, axis=-1, keepdims=True) * (1.0 / MOBA_BLOCK)
        means = jnp.where(block_lane == c * blocks_per_chunk + j, column, means)
    kmean_ref[...] = means

    @pl.when(c == n_chunks - 1)
    def _():
        sc = _dot(_head_rows(q_ref[...]), means)
        lane = lax.broadcasted_iota(jnp.int32, sc.shape, 1).astype(F32)
        out_lane = lax.broadcasted_iota(jnp.int32, (N_HEADS, LANES), 1)
        picks = jnp.zeros((N_HEADS, LANES), jnp.int32)
        for r in range(n_sel):
            best = jnp.max(sc, axis=-1, keepdims=True)
            idx = jnp.min(jnp.where(sc == best, lane, float(n_blocks)), axis=-1, keepdims=True)
            picks = jnp.where(out_lane == r, idx.astype(jnp.int32), picks)
            sc = jnp.where(lane == idx, NEG_INF, sc)
        sel_ref[...] = picks


def _moba_select(q, pool_k, page_table, layer, n_sel):
    db, n_pages = page_table.shape
    n_chunks = n_pages // CHUNK_PAGES
    n_blocks = n_pages * PAGE_SIZE // MOBA_BLOCK
    return pl.pallas_call(
        functools.partial(_moba_select_kernel, layer=layer, n_chunks=n_chunks, n_sel=n_sel),
        grid_spec=pltpu.PrefetchScalarGridSpec(
            num_scalar_prefetch=1,
            grid=(db, n_chunks),
            in_specs=[pl.BlockSpec((None, 1, WIDTH), lambda b, c, pt: (b, 0, 0)),
                      pl.BlockSpec(memory_space=pl.ANY)],
            out_specs=pl.BlockSpec((None, N_HEADS, LANES), lambda b, c, pt: (b, 0, 0)),
            scratch_shapes=[
                pltpu.VMEM((2, CHUNK_PAGES, WIDTH, PAGE_SIZE), F32),
                pltpu.SemaphoreType.DMA((2,)),
                pltpu.VMEM((WIDTH, n_blocks), F32),
            ],
        ),
        out_shape=jax.ShapeDtypeStruct((db, N_HEADS, LANES), jnp.int32),
        compiler_params=_params("arbitrary", "arbitrary"),
        name="moba_select",
    )(page_table, q.reshape(db, 1, WIDTH), _pool_view(pool_k))


def _moba_sample_kernel(pt_ref, sel_ref, slopes_ref, q_ref, kn_ref, vn_ref, kpool_ref, vpool_ref, o_ref,
                        kbuf, vbuf, sem, *, layer, n_sel, past):
    b = pl.program_id(0)
    pages_per_block = MOBA_BLOCK // PAGE_SIZE
    n_keys = n_sel * MOBA_BLOCK

    def copies():
        out = []
        for h in range(N_HEADS):
            dims = pl.ds(h * HEAD_DIM, HEAD_DIM)
            for r in range(n_sel):
                blk = sel_ref[b, h * n_sel + r]
                for j in range(pages_per_block):
                    page = pt_ref[b, blk * pages_per_block + j]
                    keys = pl.ds((r * pages_per_block + j) * PAGE_SIZE, PAGE_SIZE)
                    out.append(pltpu.make_async_copy(kpool_ref.at[layer, page, dims], kbuf.at[h, :, keys], sem.at[0]))
                    out.append(pltpu.make_async_copy(vpool_ref.at[layer, page, dims], vbuf.at[h, :, keys], sem.at[1]))
        return out

    for cp in copies():
        cp.start()
    for cp in copies():
        cp.wait()

    col = lax.broadcasted_iota(jnp.int32, (1, n_keys), 1)
    rank_of = col // MOBA_BLOCK
    head_row = lax.broadcasted_iota(jnp.int32, (N_HEADS, HEAD_DIM), 0)
    q_all = q_ref[...].astype(BF16).astype(F32)
    k_new = kn_ref[...].astype(BF16).astype(F32)
    v_new = vn_ref[...].astype(BF16).astype(F32)
    out = jnp.zeros((N_HEADS, HEAD_DIM), F32)
    for h in range(N_HEADS):
        qh = q_all[h:h + 1]
        q_rows = jnp.broadcast_to(qh, (N_HEADS, HEAD_DIM)).astype(BF16)
        slope = slopes_ref[h]
        z = jnp.dot(q_rows, kbuf[h].astype(BF16), preferred_element_type=F32)[0:1]
        blk = jnp.zeros((1, n_keys), jnp.int32)
        for r in range(n_sel):
            blk = jnp.where(rank_of == r, sel_ref[b, h * n_sel + r], blk)
        pos = blk * MOBA_BLOCK + col % MOBA_BLOCK
        z = z - slope * (past - pos).astype(F32)
        z_own = jnp.sum(qh * k_new[h:h + 1], axis=-1, keepdims=True)
        m = jnp.maximum(jnp.max(z, axis=-1, keepdims=True), z_own)
        e = jnp.exp(z - m)
        e_own = jnp.exp(z_own - m)
        denom = jnp.sum(e, axis=-1, keepdims=True) + e_own
        pv = lax.dot_general(jnp.broadcast_to(e, (N_HEADS, n_keys)).astype(BF16), vbuf[h].astype(BF16), NT_DIMS,
                             preferred_element_type=F32)[0:1]
        o_h = (pv + e_own * v_new[h:h + 1]) / denom
        out = jnp.where(head_row == h, jnp.broadcast_to(o_h, out.shape), out)
    o_ref[...] = out


def _moba_sample(q, k_new, v_new, pool_k, pool_v, page_table, sel, layer, n_sel):
    db, n_pages = page_table.shape
    past = n_pages * PAGE_SIZE
    assert past % MOBA_BLOCK == 0, "own block must hold no cached rows"
    heads = pl.BlockSpec((None, N_HEADS, HEAD_DIM), lambda b, pt, s, sl: (b, 0, 0))
    as_heads = lambda x: x.reshape(db, N_HEADS, HEAD_DIM)
    return pl.pallas_call(
        functools.partial(_moba_sample_kernel, layer=layer, n_sel=n_sel, past=past),
        grid_spec=pltpu.PrefetchScalarGridSpec(
            num_scalar_prefetch=3,
            grid=(db,),
            in_specs=[heads, heads, heads,
                      pl.BlockSpec(memory_space=pl.ANY),
                      pl.BlockSpec(memory_space=pl.ANY)],
            out_specs=heads,
            scratch_shapes=[
                pltpu.VMEM((N_HEADS, HEAD_DIM, n_sel * MOBA_BLOCK), F32),
                pltpu.VMEM((N_HEADS, HEAD_DIM, n_sel * MOBA_BLOCK), F32),
                pltpu.SemaphoreType.DMA((2,)),
            ],
        ),
        out_shape=jax.ShapeDtypeStruct((db, N_HEADS, HEAD_DIM), F32),
        compiler_params=_params("arbitrary"),
        name="moba_sample",
    )(page_table, sel, _alibi_slopes(), as_heads(q), as_heads(k_new), as_heads(v_new),
      _pool_view(pool_k), _pool_view(pool_v))


def kernel(x_prompt, x_sample, cache_sb_k, cache_sb_v, cache_moba_k, cache_moba_v, page_table, ffn1_norm_pre, ffn1_norm_post, ffn1_w_up, ffn1_w_down, mix_norm_pre, mix_norm_post, w_in, b_gate, w_branch_sb, w_branch_moba, w_out, ffn2_norm_pre, ffn2_norm_post, ffn2_w_up, ffn2_w_down):
    batch, seq, d = x_prompt.shape
    db, dec_seq, _ = x_sample.shape
    assert dec_seq == 1, "the sample kernels take one new token per sequence"
    depth = w_in.shape[0]
    n_pages = page_table.shape[1]
    n_full = n_pages * PAGE_SIZE // MOBA_BLOCK
    n_sel = min(MOBA_TOPK, n_full)
    assert n_sel >= 1

    hp = x_prompt.reshape(batch * seq, d)
    hs = x_sample.reshape(db * dec_seq, d)
    carried = [jnp.zeros((depth, batch, WIDTH, seq), F32) for _ in range(4)]
    rows_s = [[], [], [], []]
    for l in range(depth):
        w1u, w1d = ffn1_w_up[l].astype(BF16), ffn1_w_down[l].astype(BF16)
        w2u, w2d = ffn2_w_up[l].astype(BF16), ffn2_w_down[l].astype(BF16)
        wi = w_in[l].astype(BF16)
        wbs, wbm, wo = w_branch_sb[l].astype(BF16), w_branch_moba[l].astype(BF16), w_out[l].astype(BF16)

        hp = _ffn(hp, ffn1_norm_pre[l], ffn1_norm_post[l], w1u, w1d)
        qa, kab, vab, qm, kmb, vmb, gates, kmean, *carried = _in_proj(
            hp, mix_norm_pre[l], wi, layer=l, depth=depth, batch=batch, carried=tuple(carried))
        o_sb = _sb_prompt(qa, kab, vab, batch, seq)
        o_m = _moba_prompt(qm, kmb, vmb, kmean, batch, seq)
        hp = _merge(hp, o_sb, o_m, gates, b_gate[l], wbs, wbm, wo, mix_norm_post[l])
        hp = _ffn(hp, ffn2_norm_pre[l], ffn2_norm_post[l], w2u, w2d)

        hs = _ffn(hs, ffn1_norm_pre[l], ffn1_norm_post[l], ffn1_w_up[l], ffn1_w_down[l])
        qa, ka, va, qm, km, vm, gates = _in_proj(hs, mix_norm_pre[l], w_in[l])
        o_sb = _sb_sample(qa, ka, va, cache_sb_k, cache_sb_v, page_table, l)
        sel = _moba_select(qm, cache_moba_k, page_table, l, n_sel)[:, :, :n_sel].reshape(db, N_HEADS * n_sel)
        o_m = _moba_sample(qm, km, vm, cache_moba_k, cache_moba_v, page_table, sel, l, n_sel)
        hs = _merge(hs, o_sb.reshape(db, WIDTH), o_m.reshape(db, WIDTH), gates, b_gate[l],
                    w_branch_sb[l], w_branch_moba[l], w_out[l], mix_norm_post[l])
        hs = _ffn(hs, ffn2_norm_pre[l], ffn2_norm_post[l], ffn2_w_up[l], ffn2_w_down[l])
        for acc, r in zip(rows_s, (ka, va, km, vm)):
            acc.append(r.reshape(db, dec_seq, N_HEADS, HEAD_DIM))

    rows_p = [t.reshape(depth, batch, N_HEADS, HEAD_DIM, seq).transpose(0, 1, 4, 2, 3) for t in carried]
    return (hp.reshape(batch, seq, d), hs.reshape(db, dec_seq, d),
            *rows_p, *(jnp.stack(r) for r in rows_s))
```

```python
import functools

import numpy as np
import jax
import jax.numpy as jnp
from jax import lax
from jax.experimental import pallas as pl
from jax.experimental.pallas import tpu as pltpu

HEAD_DIM = 64
N_HEADS = 8
WIDTH = N_HEADS * HEAD_DIM
PAIR = 2 * HEAD_DIM
N_PAIRS = N_HEADS // 2
MOBA_BLOCK = 256
MOBA_TOPK = 3
PAGE_SIZE = 128
NORM_EPS = 1e-6
LANES = 128
VMEM_LIMIT = 56 * 1024 * 1024

F32 = jnp.float32
BF16 = jnp.bfloat16
NEG_INF = float("-inf")
NT_DIMS = (((1,), (1,)), ((), ()))


def _params(*semantics):
    return pltpu.CompilerParams(dimension_semantics=semantics, vmem_limit_bytes=VMEM_LIMIT)


def _rms(x, g):
    return x * lax.rsqrt(jnp.mean(x * x, axis=-1, keepdims=True) + NORM_EPS) * g


def _resident(shape, index_map):
    return pl.BlockSpec(shape, index_map, pipeline_mode=pl.Buffered(1))


FFN_CHUNK = 256


def _dot(a, b, dims=None):
    precision = lax.Precision.HIGHEST if a.dtype == F32 else None
    if dims is None:
        return jnp.dot(a, b, preferred_element_type=F32, precision=precision)
    return lax.dot_general(a, b, dims, preferred_element_type=F32, precision=precision)


def _ffn_kernel(x_ref, gpre_ref, gpost_ref, wa_ref, wb_ref, wd_ref, o_ref, acc_ref, u_ref):
    j = pl.program_id(1)

    @pl.when(j == 0)
    def _():
        u_ref[...] = _rms(x_ref[...], gpre_ref[...]).astype(u_ref.dtype)
        acc_ref[...] = jnp.zeros_like(acc_ref)

    u = u_ref[...]
    for c in range(wd_ref.shape[0] // FFN_CHUNK):
        cols = slice(c * FFN_CHUNK, (c + 1) * FFN_CHUNK)
        a = _dot(u, wa_ref[:, cols])
        b = _dot(u, wb_ref[:, cols])
        mid = (a * jax.nn.sigmoid(a) * b).astype(u.dtype)
        acc_ref[...] += _dot(mid, wd_ref[cols, :])

    @pl.when(j == pl.num_programs(1) - 1)
    def _():
        o_ref[...] = x_ref[...] + 0.5 * _rms(acc_ref[...], gpost_ref[...])


def _ffn(x, g_pre, g_post, w_up, w_down):
    m, d = x.shape
    f = w_down.shape[0]
    assert f % FFN_CHUNK == 0 and w_up.shape == (d, 2 * f)
    tm = min(512, m)
    assert m % tm == 0
    if w_up.dtype == BF16:
        tf, spec = f, _resident
    else:
        tf, spec = FFN_CHUNK, pl.BlockSpec
    n_slabs = f // tf
    return pl.pallas_call(
        _ffn_kernel,
        grid=(m // tm, n_slabs),
        in_specs=[
            pl.BlockSpec((tm, d), lambda i, j: (i, 0)),
            _resident((1, d), lambda i, j: (0, 0)),
            _resident((1, d), lambda i, j: (0, 0)),
            spec((d, tf), lambda i, j: (0, j)),
            spec((d, tf), lambda i, j: (0, n_slabs + j)),
            spec((tf, d), lambda i, j: (j, 0)),
        ],
        out_specs=pl.BlockSpec((tm, d), lambda i, j: (i, 0)),
        out_shape=jax.ShapeDtypeStruct((m, d), F32),
        scratch_shapes=[pltpu.VMEM((tm, d), F32), pltpu.VMEM((tm, d), w_up.dtype)],
        compiler_params=_params("parallel", "arbitrary"),
        name="ffn",
    )(x, g_pre.reshape(1, d), g_post.reshape(1, d), w_up, w_up, w_down)


def _in_kernel(*refs, prompt, n_carried):
    h_ref, g_ref, w_ref = refs[:3]
    outs = refs[3 + n_carried:]
    qa_ref, ka_ref, va_ref, qm_ref, km_ref, vm_ref, gate_ref = outs[:7]
    if prompt:
        kmean_ref, kat_ref, vat_ref, kmt_ref, vmt_ref = outs[7:]
    else:
        kat_ref = vat_ref = kmt_ref = vmt_ref = None
    u = _rms(h_ref[...], g_ref[...]).astype(w_ref.dtype)
    scale = HEAD_DIM ** -0.5

    def proj(c):
        return _dot(u, w_ref[:, c * WIDTH:(c + 1) * WIDTH])

    def emit(c, row_ref, t_ref):
        r = proj(c)
        row_ref[...] = r.astype(row_ref.dtype)
        if prompt:
            t_ref[...] = r.T
        return r

    qa_ref[...] = (proj(0) * scale).astype(qa_ref.dtype)
    emit(1, ka_ref, kat_ref)
    emit(2, va_ref, vat_ref)
    qm_ref[...] = (proj(3) * scale).astype(qm_ref.dtype)
    r = emit(4, km_ref, kmt_ref)
    if prompt:
        for j in range(r.shape[0] // MOBA_BLOCK):
            blk = r[j * MOBA_BLOCK:(j + 1) * MOBA_BLOCK]
            kmean_ref[j] = jnp.sum(blk, axis=0, keepdims=True) * (1.0 / MOBA_BLOCK)
    emit(5, vm_ref, vmt_ref)
    for c in range(gate_ref.shape[1] // WIDTH):
        gate_ref[:, c * WIDTH:(c + 1) * WIDTH] = proj(6 + c)


def _in_proj(h, g, w_in, *, layer=0, depth=1, batch=None, carried=()):
    m, d = h.shape
    n_cols = w_in.shape[1]
    assert n_cols == 6 * WIDTH + 2 * d
    prompt = batch is not None
    tm = min(512, m)
    assert m % tm == 0
    row = lambda i: (i, 0)
    wide = pl.BlockSpec((tm, WIDTH), row)
    kv_dtype = BF16 if prompt else F32
    out_specs = [wide] * 6 + [pl.BlockSpec((tm, 2 * d), row)]
    out_shape = [jax.ShapeDtypeStruct((m, WIDTH), kv_dtype)] * 6
    out_shape.append(jax.ShapeDtypeStruct((m, 2 * d), F32))
    aliases = {}
    if prompt:
        seq = m // batch
        assert seq % tm == 0 and tm % MOBA_BLOCK == 0
        tiles = seq // tm
        out_specs.append(pl.BlockSpec((tm // MOBA_BLOCK, 1, WIDTH), lambda i: (i, 0, 0)))
        out_shape.append(jax.ShapeDtypeStruct((m // MOBA_BLOCK, 1, WIDTH), F32))
        out_specs += [pl.BlockSpec((None, None, WIDTH, tm), lambda i: (layer, i // tiles, 0, i % tiles))] * 4
        out_shape += [jax.ShapeDtypeStruct((depth, batch, WIDTH, seq), F32)] * 4
        assert len(carried) == 4
        aliases = {3 + j: 8 + j for j in range(4)}
    return pl.pallas_call(
        functools.partial(_in_kernel, prompt=prompt, n_carried=len(carried)),
        grid=(m // tm,),
        in_specs=[
            pl.BlockSpec((tm, d), row),
            _resident((1, d), lambda i: (0, 0)),
            _resident((d, n_cols), lambda i: (0, 0)),
        ] + [pl.BlockSpec(memory_space=pl.ANY)] * len(carried),
        out_specs=out_specs,
        out_shape=out_shape,
        input_output_aliases=aliases,
        compiler_params=_params("parallel"),
        name="in_proj",
    )(h, g.reshape(1, d), w_in, *carried)


def _merge_kernel(h_ref, osb_ref, om_ref, gate_ref, bg_ref, wbs_ref, wbm_ref, wo_ref, gpost_ref, o_ref):
    d = h_ref.shape[1]
    a = _dot(osb_ref[...], wbs_ref[...])
    b = _dot(om_ref[...], wbm_ref[...])
    merged = (jax.nn.sigmoid(gate_ref[:, :d] + bg_ref[:, :d]) * a
              + jax.nn.sigmoid(gate_ref[:, d:] + bg_ref[:, d:]) * b)
    y = _dot(merged.astype(wo_ref.dtype), wo_ref[...])
    o_ref[...] = h_ref[...] + _rms(y, gpost_ref[...])


def _merge(h, o_sb, o_m, gates, b_gate, w_bs, w_bm, w_out, g_post):
    m, d = h.shape
    tm = min(512, m)
    row = lambda i: (i, 0)
    const = lambda i: (0, 0)
    return pl.pallas_call(
        _merge_kernel,
        grid=(m // tm,),
        in_specs=[
            pl.BlockSpec((tm, d), row),
            pl.BlockSpec((tm, WIDTH), row),
            pl.BlockSpec((tm, WIDTH), row),
            pl.BlockSpec((tm, 2 * d), row),
            _resident((1, 2 * d), const),
            _resident((WIDTH, d), const),
            _resident((WIDTH, d), const),
            _resident((d, d), const),
            _resident((1, d), const),
        ],
        out_specs=pl.BlockSpec((tm, d), row),
        out_shape=jax.ShapeDtypeStruct((m, d), F32),
        compiler_params=_params("parallel"),
        name="merge",
    )(h, o_sb, o_m, gates, b_gate.reshape(1, 2 * d), w_bs, w_bm, w_out, g_post.reshape(1, d))


def _suffix_matrix(n):
    j = np.arange(n)[:, None]
    s = np.arange(n)[None, :]
    u1 = np.concatenate([(j > s), np.ones((n, n), bool)], axis=1)
    return jnp.asarray(np.concatenate([u1, u1], axis=0), dtype=BF16)


SB_DEAD = -120.0


def _log_sigmoids(z):
    log_beta = jnp.minimum(z, 0.0) - jnp.log(1.0 + jnp.exp(-jnp.abs(z)))
    return log_beta, log_beta - z


def _suffix_sums(x, uo):
    hi = x.astype(BF16)
    lo = (x - hi.astype(F32)).astype(BF16)
    t = jnp.dot(jnp.concatenate([hi, lo], axis=1), uo, preferred_element_type=F32)
    n = x.shape[1]
    return t[:, :n], t[:, n:]


SB_Q_TILE = 256
SB_K_TILE = 128


def _sb_prompt_kernel(q_ref, k_ref, v_ref, uo_ref, o_ref, carry_ref, acc_ref):
    i = pl.program_id(2)
    tq, tk = SB_Q_TILE, SB_K_TILE
    per = tq // tk
    q = q_ref[...]
    uo = uo_ref[...]
    head0 = (lax.broadcasted_iota(jnp.int32, (1, PAIR), 1) // HEAD_DIM) == 0
    row = lax.broadcasted_iota(jnp.int32, (tq, tk), 0)
    col = lax.broadcasted_iota(jnp.int32, (tq, tk), 1)
    carry_ref[...] = jnp.zeros_like(carry_ref)
    acc_ref[...] = jnp.zeros_like(acc_ref)

    def by_head(x):
        zero = jnp.zeros_like(x)
        return jnp.concatenate([jnp.where(head0, x, zero), jnp.where(head0, zero, x)], axis=0)

    def sweep(key_tiles, masks):
        pending = []
        for jt, mask in zip(key_tiles, masks):
            off = pl.multiple_of(jt * tk, tk)
            z2 = lax.dot_general(q, by_head(k_ref[pl.ds(off, tk), :]), NT_DIMS, preferred_element_type=F32)
            partial, total = [], []
            for hh in range(2):
                log_beta, log_keep = _log_sigmoids(z2[:, hh * tk:(hh + 1) * tk])
                if mask is not None:
                    log_keep = jnp.where(mask, log_keep, 0.0)
                later, tot = _suffix_sums(log_keep, uo)
                partial.append(log_beta + later)
                total.append(tot)
            pending.append((off, mask, partial, total))
        for off, mask, partial, total in pending:
            weights = []
            for hh in range(2):
                carry = carry_ref[hh]
                a = jnp.exp(partial[hh] + carry)
                if mask is not None:
                    a = jnp.where(mask, a, 0.0)
                weights.append(a.astype(BF16))
                carry_ref[hh] = carry + total[hh]
            acc_ref[...] += jnp.dot(jnp.concatenate(weights, axis=1), by_head(v_ref[pl.ds(off, tk), :]),
                                    preferred_element_type=F32)

    inner = list(reversed(range(per)))
    sweep([i * per + d for d in inner], [col + d * tk < row for d in inner])

    def past(state):
        s, _ = state
        top = (i - s) * per - 1
        sweep([top - d for d in range(per)], [None] * per)
        alive = jnp.max(jnp.maximum(carry_ref[0], carry_ref[1])) > SB_DEAD
        return s + 1, alive.astype(jnp.int32)

    lax.while_loop(lambda st: (st[0] < i) & (st[1] > 0), past, (jnp.int32(0), jnp.int32(1)))
    o_ref[...] = acc_ref[...].astype(BF16)


def _sb_prompt(q, kb, vb, batch, seq):
    tq, tk = SB_Q_TILE, SB_K_TILE
    nq = seq // tq
    assert seq % tq == 0
    return pl.pallas_call(
        _sb_prompt_kernel,
        grid=(batch, N_PAIRS, nq),
        in_specs=[
            pl.BlockSpec((tq, PAIR), lambda b, p, i: (b * nq + i, p)),
            pl.BlockSpec((seq, PAIR), lambda b, p, i: (b, p)),
            pl.BlockSpec((seq, PAIR), lambda b, p, i: (b, p)),
            _resident((2 * tk, 2 * tk), lambda b, p, i: (0, 0)),
        ],
        out_specs=pl.BlockSpec((tq, PAIR), lambda b, p, i: (b * nq + i, p)),
        out_shape=jax.ShapeDtypeStruct((batch * seq, WIDTH), BF16),
        scratch_shapes=[pltpu.VMEM((2, tq, tk), F32), pltpu.VMEM((tq, PAIR), F32)],
        compiler_params=_params("parallel", "parallel", "parallel"),
        name="sb_prompt",
    )(q, kb, vb, _suffix_matrix(tk))


def _alibi_slopes():
    return jnp.asarray(2.0 ** (-8.0 * np.arange(1, N_HEADS + 1) / N_HEADS), dtype=F32)


MASKED = -1e30


def _moba_prompt_kernel(slopes_ref, q_ref, k_ref, v_ref, kma_ref, kmb_ref, o_ref,
                        s_ref, mx_ref, ls_ref, acc_ref, *, nb):
    p = pl.program_id(1)
    i = pl.program_id(2)
    t = MOBA_BLOCK
    q = q_ref[...]
    lane = lax.broadcasted_iota(jnp.int32, (1, PAIR), 1)
    row = lax.broadcasted_iota(jnp.int32, (t, t), 0)
    col = lax.broadcasted_iota(jnp.int32, (t, t), 1)
    row_minus_col = (row - col).astype(F32)
    causal = col <= row
    pl_lane = lax.broadcasted_iota(jnp.int32, (t, LANES), 1)
    m_of = pl_lane // nb
    n_of = pl_lane % nb
    pair_ok = (pl_lane < nb * nb) & (m_of < i)
    gl = lax.broadcasted_iota(jnp.int32, (LANES, LANES), 0)
    gn = lax.broadcasted_iota(jnp.int32, (LANES, LANES), 1)
    group = jnp.where((gl < nb * nb) & (gl % nb == gn), 1.0, 0.0).astype(BF16)
    pick_row = lax.broadcasted_iota(jnp.int32, (LANES, t), 0)
    qh, slope, unselected = [], [], []
    for hh in range(2):
        qh.append(jnp.where((lane // HEAD_DIM) == hh, q, jnp.zeros_like(q)))
        slope.append(slopes_ref[2 * p + hh])
        sa = lax.dot_general(qh[hh], kma_ref[...], NT_DIMS, preferred_element_type=F32)
        sb = lax.dot_general(qh[hh], kmb_ref[...], NT_DIMS, preferred_element_type=F32)
        beats = ((sa > sb) | ((sa == sb) & (m_of < n_of))) & pair_ok
        rank = jnp.dot(jnp.where(beats, 1.0, 0.0).astype(BF16), group, preferred_element_type=F32)
        unselected.append(jnp.where((rank < MOBA_TOPK) & (pl_lane < i), 0.0, 1.0).astype(BF16))

    def logits(hh, n):
        off = pl.multiple_of(n * t, t)
        z = lax.dot_general(qh[hh], k_ref[pl.ds(off, t), :], NT_DIMS, preferred_element_type=F32)
        return z - slope[hh] * (row_minus_col + ((i - n) * t).astype(F32))

    for hh in range(2):
        z = jnp.where(causal, logits(hh, i), MASKED)
        s_ref[hh, i] = z
        mx_ref[hh] = z

    def scan(n, _):
        pick = jnp.where(pick_row == n, MASKED, 0.0).astype(BF16)
        for hh in range(2):
            z = logits(hh, n) + jnp.dot(unselected[hh], pick, preferred_element_type=F32)
            s_ref[hh, n] = z
            mx_ref[hh] = jnp.maximum(mx_ref[hh], z)
        return 0

    lax.fori_loop(0, i, scan, 0)

    m = [jnp.max(mx_ref[hh], axis=-1, keepdims=True) for hh in range(2)]
    ls_ref[...] = jnp.zeros_like(ls_ref)
    acc_ref[...] = jnp.zeros_like(acc_ref)

    def accumulate(n, _):
        v = v_ref[pl.ds(pl.multiple_of(n * t, t), t), :]
        for hh in range(2):
            e = jnp.exp(s_ref[hh, n] - m[hh])
            ls_ref[hh] += e
            acc_ref[hh] += jnp.dot(e.astype(BF16), v, preferred_element_type=F32)
        return 0

    lax.fori_loop(0, i + 1, accumulate, 0)
    o = [acc_ref[hh] / jnp.sum(ls_ref[hh], axis=-1, keepdims=True) for hh in range(2)]
    o_ref[...] = jnp.where((lane // HEAD_DIM) == 0, o[0], o[1]).astype(BF16)


def _moba_prompt(q, kb, vb, kmean, batch, seq):
    t = MOBA_BLOCK
    nb = seq // t
    assert seq % t == 0 and nb * nb <= LANES
    km = kmean.reshape(batch, nb, WIDTH).astype(BF16)
    pad = jnp.zeros((batch, LANES - nb * nb, WIDTH), BF16)
    kma = jnp.concatenate([jnp.repeat(km, nb, axis=1), pad], axis=1).reshape(batch * LANES, WIDTH)
    kmb = jnp.concatenate([jnp.tile(km, (1, nb, 1)), pad], axis=1).reshape(batch * LANES, WIDTH)
    return pl.pallas_call(
        functools.partial(_moba_prompt_kernel, nb=nb),
        grid_spec=pltpu.PrefetchScalarGridSpec(
            num_scalar_prefetch=1,
            grid=(batch, N_PAIRS, nb),
            in_specs=[
                pl.BlockSpec((t, PAIR), lambda b, p, i, s: (b * nb + i, p)),
                pl.BlockSpec((seq, PAIR), lambda b, p, i, s: (b, p)),
                pl.BlockSpec((seq, PAIR), lambda b, p, i, s: (b, p)),
                pl.BlockSpec((LANES, PAIR), lambda b, p, i, s: (b, p)),
                pl.BlockSpec((LANES, PAIR), lambda b, p, i, s: (b, p)),
            ],
            out_specs=pl.BlockSpec((t, PAIR), lambda b, p, i, s: (b * nb + i, p)),
            scratch_shapes=[
                pltpu.VMEM((2, nb, t, t), F32),
                pltpu.VMEM((2, t, t), F32),
                pltpu.VMEM((2, t, t), F32),
                pltpu.VMEM((2, t, PAIR), F32),
            ],
        ),
        out_shape=jax.ShapeDtypeStruct((batch * seq, WIDTH), BF16),
        compiler_params=_params("parallel", "parallel", "parallel"),
        name="moba_prompt",
    )(_alibi_slopes(), q, kb, vb, kma, kmb)


CHUNK_PAGES = 16
CHUNK_KEYS = CHUNK_PAGES * PAGE_SIZE
SUB = 256


def _page_copies(pt_ref, pool_ref, buf_ref, sem_ref, layer, step, slot, n_chunks, descending):
    b = step // n_chunks
    c = step % n_chunks
    first_page = ((n_chunks - 1 - c) if descending else c) * CHUNK_PAGES
    return [
        pltpu.make_async_copy(pool_ref.at[layer, pt_ref[b, first_page + j]], buf_ref.at[slot, j], sem_ref.at[slot])
        for j in range(CHUNK_PAGES)
    ]


def _stream_chunk(pt_ref, pools_bufs_sems, layer, n_chunks, descending):
    step = pl.program_id(0) * n_chunks + pl.program_id(1)
    n_steps = pl.num_programs(0) * n_chunks
    slot = step % 2

    def start(s, sl):
        for pool_ref, buf_ref, sem_ref in pools_bufs_sems:
            for cp in _page_copies(pt_ref, pool_ref, buf_ref, sem_ref, layer, s, sl, n_chunks, descending):
                cp.start()

    @pl.when(step == 0)
    def _():
        start(step, slot)

    @pl.when(step + 1 < n_steps)
    def _():
        start(step + 1, 1 - slot)

    for pool_ref, buf_ref, sem_ref in pools_bufs_sems:
        for cp in _page_copies(pt_ref, pool_ref, buf_ref, sem_ref, layer, step, slot, n_chunks, descending):
            cp.wait()
    return slot


def _head_rows(x):
    rows = lax.broadcasted_iota(jnp.int32, (N_HEADS, WIDTH), 0)
    lanes = lax.broadcasted_iota(jnp.int32, (N_HEADS, WIDTH), 1)
    xb = jnp.broadcast_to(x.astype(F32), (N_HEADS, WIDTH))
    return jnp.where(lanes // HEAD_DIM == rows, xb, 0.0)


SB_SAMPLE_PAGES = 4
SB_SAMPLE_SLOTS = 4


def _sb_sample_kernel(pt_ref, q_ref, kn_ref, vn_ref, uo_ref, kpool_ref, vpool_ref, o_ref,
                      kbuf, vbuf, ksem, vsem, carry_ref, acc_ref, *, layer, n_chunks):
    b = pl.program_id(0)
    chunk_keys = SB_SAMPLE_PAGES * PAGE_SIZE

    def copies(sample, c):
        slot = jnp.where(c == 0, 2 + sample % 2, c % 2)
        first_page = (n_chunks - 1 - c) * SB_SAMPLE_PAGES
        out = []
        for j in range(SB_SAMPLE_PAGES):
            page = pt_ref[sample, first_page + j]
            out.append(pltpu.make_async_copy(kpool_ref.at[layer, page], kbuf.at[slot, j], ksem.at[slot]))
            out.append(pltpu.make_async_copy(vpool_ref.at[layer, page], vbuf.at[slot, j], vsem.at[slot]))
        return slot, out

    def start(sample, c):
        for cp in copies(sample, c)[1]:
            cp.start()

    def wait(sample, c):
        slot, out = copies(sample, c)
        for cp in out:
            cp.wait()
        return slot

    @pl.when(b == 0)
    def _():
        start(b, 0)

    @pl.when(b + 1 < pl.num_programs(0))
    def _():
        start(b + 1, 0)

    qb = _head_rows(q_ref[...]).astype(BF16)
    uo = uo_ref[...]
    t_new = lax.broadcasted_iota(jnp.int32, (1, 1), 0)
    allowed = t_new < t_new
    z_new = jnp.sum(qb.astype(F32) * kn_ref[...].astype(BF16).astype(F32), axis=-1, keepdims=True)
    log_beta_new, log_keep_new = _log_sigmoids(z_new)
    carry_ref[...] = jnp.broadcast_to(jnp.where(allowed, log_keep_new, 0.0), carry_ref.shape)
    acc_ref[...] = jnp.where(allowed, jnp.exp(log_beta_new), 0.0) * _head_rows(vn_ref[...].astype(BF16).astype(F32))

    def sweep_chunk(state):
        c, _ = state

        @pl.when(c + 1 < n_chunks)
        def _():
            start(b, c + 1)

        slot = wait(b, c)
        z = jnp.concatenate([jnp.dot(qb, kbuf[slot, j].astype(BF16), preferred_element_type=F32)
                             for j in range(SB_SAMPLE_PAGES)], axis=1)
        log_beta, log_keep = _log_sigmoids(z)
        n_sub = chunk_keys // SUB
        sums = [_suffix_sums(log_keep[:, s * SUB:(s + 1) * SUB], uo) for s in range(n_sub)]
        carry = carry_ref[...]
        weights = [None] * n_sub
        for s in reversed(range(n_sub)):
            later, total = sums[s]
            weights[s] = jnp.exp(log_beta[:, s * SUB:(s + 1) * SUB] + later + carry)
            carry = carry + total
        carry_ref[...] = carry
        acc = acc_ref[...]
        pages_per_sub = SUB // PAGE_SIZE
        for j in range(SB_SAMPLE_PAGES):
            w = weights[j // pages_per_sub]
            off = (j % pages_per_sub) * PAGE_SIZE
            acc = acc + lax.dot_general(w[:, off:off + PAGE_SIZE].astype(BF16), vbuf[slot, j].astype(BF16),
                                        NT_DIMS, preferred_element_type=F32)
        acc_ref[...] = acc
        return c + 1, (jnp.max(carry) > SB_DEAD).astype(jnp.int32)

    c_end, _ = lax.while_loop(lambda st: (st[0] < n_chunks) & (st[1] > 0), sweep_chunk,
                              (jnp.int32(0), jnp.int32(1)))

    @pl.when(c_end < n_chunks)
    def _():
        wait(b, c_end)

    rows = lax.broadcasted_iota(jnp.int32, (N_HEADS, WIDTH), 0)
    lanes = lax.broadcasted_iota(jnp.int32, (N_HEADS, WIDTH), 1)
    own = jnp.where(lanes // HEAD_DIM == rows, acc_ref[...], 0.0)
    o_ref[...] = jnp.sum(own, axis=0, keepdims=True)


def _pool_view(pool):
    depth, n_pool, page, heads, dh = pool.shape
    assert page == PAGE_SIZE and heads * dh == WIDTH
    return pool.transpose(0, 1, 3, 4, 2).reshape(depth, n_pool, WIDTH, PAGE_SIZE)


def _sb_sample(q, k_new, v_new, pool_k, pool_v, page_table, layer):
    db, n_pages = page_table.shape
    assert n_pages % SB_SAMPLE_PAGES == 0 and (SB_SAMPLE_PAGES * PAGE_SIZE) % SUB == 0
    n_chunks = n_pages // SB_SAMPLE_PAGES
    row3 = pl.BlockSpec((None, 1, WIDTH), lambda b, pt: (b, 0, 0))
    slots = (SB_SAMPLE_SLOTS, SB_SAMPLE_PAGES, WIDTH, PAGE_SIZE)
    return pl.pallas_call(
        functools.partial(_sb_sample_kernel, layer=layer, n_chunks=n_chunks),
        grid_spec=pltpu.PrefetchScalarGridSpec(
            num_scalar_prefetch=1,
            grid=(db,),
            in_specs=[row3, row3, row3,
                      _resident((2 * SUB, 2 * SUB), lambda b, pt: (0, 0)),
                      pl.BlockSpec(memory_space=pl.ANY),
                      pl.BlockSpec(memory_space=pl.ANY)],
            out_specs=row3,
            scratch_shapes=[
                pltpu.VMEM(slots, F32),
                pltpu.VMEM(slots, F32),
                pltpu.SemaphoreType.DMA((SB_SAMPLE_SLOTS,)),
                pltpu.SemaphoreType.DMA((SB_SAMPLE_SLOTS,)),
                pltpu.VMEM((N_HEADS, SUB), F32),
                pltpu.VMEM((N_HEADS, WIDTH), F32),
            ],
        ),
        out_shape=jax.ShapeDtypeStruct((db, 1, WIDTH), F32),
        compiler_params=_params("arbitrary"),
        name="sb_sample",
    )(page_table, q.reshape(db, 1, WIDTH), k_new.reshape(db, 1, WIDTH), v_new.reshape(db, 1, WIDTH),
      _suffix_matrix(SUB), _pool_view(pool_k), _pool_view(pool_v))


def _moba_select_kernel(pt_ref, q_ref, kpool_ref, sel_ref, kbuf, ksem, kmean_ref, *, layer, n_chunks, n_sel):
    c = pl.program_id(1)
    slot = _stream_chunk(pt_ref, [(kpool_ref, kbuf, ksem)], layer, n_chunks, False)
    pages_per_block = MOBA_BLOCK // PAGE_SIZE
    blocks_per_chunk = CHUNK_PAGES // pages_per_block
    n_blocks = kmean_ref.shape[1]

    @pl.when(c == 0)
    def _():
        kmean_ref[...] = jnp.zeros_like(kmean_ref)

    block_lane = lax.broadcasted_iota(jnp.int32, kmean_ref.shape, 1)
    means = kmean_ref[...]
    for j in range(blocks_per_chunk):
        total = kbuf[slot, j * pages_per_block]
        for t in range(1, pages_per_block):
            total = total + kbuf[slot, j * pages_per_block + t]
        column = jnp.sum(total, axis=-1, keepdims=True) * (1.0 / MOBA_BLOCK)
        means = jnp.where(block_lane == c * blocks_per_chunk + j, column, means)
    kmean_ref[...] = means

    @pl.when(c == n_chunks - 1)
    def _():
        sc = _dot(_head_rows(q_ref[...]), means)
        lane = lax.broadcasted_iota(jnp.int32, sc.shape, 1).astype(F32)
        out_lane = lax.broadcasted_iota(jnp.int32, (N_HEADS, LANES), 1)
        picks = jnp.zeros((N_HEADS, LANES), jnp.int32)
        for r in range(n_sel):
            best = jnp.max(sc, axis=-1, keepdims=True)
            idx = jnp.min(jnp.where(sc == best, lane, float(n_blocks)), axis=-1, keepdims=True)
            picks = jnp.where(out_lane == r, idx.astype(jnp.int32), picks)
            sc = jnp.where(lane == idx, NEG_INF, sc)
        sel_ref[...] = picks


def _moba_select(q, pool_k, page_table, layer, n_sel):
    db, n_pages = page_table.shape
    n_chunks = n_pages // CHUNK_PAGES
    n_blocks = n_pages * PAGE_SIZE // MOBA_BLOCK
    return pl.pallas_call(
        functools.partial(_moba_select_kernel, layer=layer, n_chunks=n_chunks, n_sel=n_sel),
        grid_spec=pltpu.PrefetchScalarGridSpec(
            num_scalar_prefetch=1,
            grid=(db, n_chunks),
            in_specs=[pl.BlockSpec((None, 1, WIDTH), lambda b, c, pt: (b, 0, 0)),
                      pl.BlockSpec(memory_space=pl.ANY)],
            out_specs=pl.BlockSpec((None, N_HEADS, LANES), lambda b, c, pt: (b, 0, 0)),
            scratch_shapes=[
                pltpu.VMEM((2, CHUNK_PAGES, WIDTH, PAGE_SIZE), F32),
                pltpu.SemaphoreType.DMA((2,)),
                pltpu.VMEM((WIDTH, n_blocks), F32),
            ],
        ),
        out_shape=jax.ShapeDtypeStruct((db, N_HEADS, LANES), jnp.int32),
        compiler_params=_params("arbitrary", "arbitrary"),
        name="moba_select",
    )(page_table, q.reshape(db, 1, WIDTH), _pool_view(pool_k))


def _moba_sample_kernel(pt_ref, sel_ref, slopes_ref, q_ref, kn_ref, vn_ref, kpool_ref, vpool_ref, o_ref,
                        kbuf, vbuf, sem, *, layer, n_sel, past):
    b = pl.program_id(0)
    pages_per_block = MOBA_BLOCK // PAGE_SIZE
    n_keys = n_sel * MOBA_BLOCK

    def copies():
        out = []
        for h in range(N_HEADS):
            dims = pl.ds(h * HEAD_DIM, HEAD_DIM)
            for r in range(n_sel):
                blk = sel_ref[b, h * n_sel + r]
                for j in range(pages_per_block):
                    page = pt_ref[b, blk * pages_per_block + j]
                    keys = pl.ds((r * pages_per_block + j) * PAGE_SIZE, PAGE_SIZE)
                    out.append(pltpu.make_async_copy(kpool_ref.at[layer, page, dims], kbuf.at[h, :, keys], sem.at[0]))
                    out.append(pltpu.make_async_copy(vpool_ref.at[layer, page, dims], vbuf.at[h, :, keys], sem.at[1]))
        return out

    for cp in copies():
        cp.start()
    for cp in copies():
        cp.wait()

    col = lax.broadcasted_iota(jnp.int32, (1, n_keys), 1)
    rank_of = col // MOBA_BLOCK
    head_row = lax.broadcasted_iota(jnp.int32, (N_HEADS, HEAD_DIM), 0)
    q_all = q_ref[...].astype(BF16).astype(F32)
    k_new = kn_ref[...].astype(BF16).astype(F32)
    v_new = vn_ref[...].astype(BF16).astype(F32)
    out = jnp.zeros((N_HEADS, HEAD_DIM), F32)
    for h in range(N_HEADS):
        qh = q_all[h:h + 1]
        q_rows = jnp.broadcast_to(qh, (N_HEADS, HEAD_DIM)).astype(BF16)
        slope = slopes_ref[h]
        z = jnp.dot(q_rows, kbuf[h].astype(BF16), preferred_element_type=F32)[0:1]
        blk = jnp.zeros((1, n_keys), jnp.int32)
        for r in range(n_sel):
            blk = jnp.where(rank_of == r, sel_ref[b, h * n_sel + r], blk)
        pos = blk * MOBA_BLOCK + col % MOBA_BLOCK
        z = z - slope * (past - pos).astype(F32)
        z_own = jnp.sum(qh * k_new[h:h + 1], axis=-1, keepdims=True)
        m = jnp.maximum(jnp.max(z, axis=-1, keepdims=True), z_own)
        e = jnp.exp(z - m)
        e_own = jnp.exp(z_own - m)
        denom = jnp.sum(e, axis=-1, keepdims=True) + e_own
        pv = lax.dot_general(jnp.broadcast_to(e, (N_HEADS, n_keys)).astype(BF16), vbuf[h].astype(BF16), NT_DIMS,
                             preferred_element_type=F32)[0:1]
        o_h = (pv + e_own * v_new[h:h + 1]) / denom
        out = jnp.where(head_row == h, jnp.broadcast_to(o_h, out.shape), out)
    o_ref[...] = out


def _moba_sample(q, k_new, v_new, pool_k, pool_v, page_table, sel, layer, n_sel):
    db, n_pages = page_table.shape
    past = n_pages * PAGE_SIZE
    assert past % MOBA_BLOCK == 0, "own block must hold no cached rows"
    heads = pl.BlockSpec((None, N_HEADS, HEAD_DIM), lambda b, pt, s, sl: (b, 0, 0))
    as_heads = lambda x: x.reshape(db, N_HEADS, HEAD_DIM)
    return pl.pallas_call(
        functools.partial(_moba_sample_kernel, layer=layer, n_sel=n_sel, past=past),
        grid_spec=pltpu.PrefetchScalarGridSpec(
            num_scalar_prefetch=3,
            grid=(db,),
            in_specs=[heads, heads, heads,
                      pl.BlockSpec(memory_space=pl.ANY),
                      pl.BlockSpec(memory_space=pl.ANY)],
            out_specs=heads,
            scratch_shapes=[
                pltpu.VMEM((N_HEADS, HEAD_DIM, n_sel * MOBA_BLOCK), F32),
                pltpu.VMEM((N_HEADS, HEAD_DIM, n_sel * MOBA_BLOCK), F32),
                pltpu.SemaphoreType.DMA((2,)),
            ],
        ),
        out_shape=jax.ShapeDtypeStruct((db, N_HEADS, HEAD_DIM), F32),
        compiler_params=_params("arbitrary"),
        name="moba_sample",
    )(page_table, sel, _alibi_slopes(), as_heads(q), as_heads(k_new), as_heads(v_new),
      _pool_view(pool_k), _pool_view(pool_v))


def kernel(x_prompt, x_sample, cache_sb_k, cache_sb_v, cache_moba_k, cache_moba_v, page_table, ffn1_norm_pre, ffn1_norm_post, ffn1_w_up, ffn1_w_down, mix_norm_pre, mix_norm_post, w_in, b_gate, w_branch_sb, w_branch_moba, w_out, ffn2_norm_pre, ffn2_norm_post, ffn2_w_up, ffn2_w_down):
    batch, seq, d = x_prompt.shape
    db, dec_seq, _ = x_sample.shape
    assert dec_seq == 1, "the sample kernels take one new token per sequence"
    depth = w_in.shape[0]
    n_pages = page_table.shape[1]
    n_full = n_pages * PAGE_SIZE // MOBA_BLOCK
    n_sel = min(MOBA_TOPK, n_full)
    assert n_sel >= 1

    hp = x_prompt.reshape(batch * seq, d)
    hs = x_sample.reshape(db * dec_seq, d)
    carried = [jnp.zeros((depth, batch, WIDTH, seq), F32) for _ in range(4)]
    rows_s = [[], [], [], []]
    for l in range(depth):
        w1u, w1d = ffn1_w_up[l].astype(BF16), ffn1_w_down[l].astype(BF16)
        w2u, w2d = ffn2_w_up[l].astype(BF16), ffn2_w_down[l].astype(BF16)
        wi = w_in[l].astype(BF16)
        wbs, wbm, wo = w_branch_sb[l].astype(BF16), w_branch_moba[l].astype(BF16), w_out[l].astype(BF16)

        hp = _ffn(hp, ffn1_norm_pre[l], ffn1_norm_post[l], w1u, w1d)
        qa, kab, vab, qm, kmb, vmb, gates, kmean, *carried = _in_proj(
            hp, mix_norm_pre[l], wi, layer=l, depth=depth, batch=batch, carried=tuple(carried))
        o_sb = _sb_prompt(qa, kab, vab, batch, seq)
        o_m = _moba_prompt(qm, kmb, vmb, kmean, batch, seq)
        hp = _merge(hp, o_sb, o_m, gates, b_gate[l], wbs, wbm, wo, mix_norm_post[l])
        hp = _ffn(hp, ffn2_norm_pre[l], ffn2_norm_post[l], w2u, w2d)

        hs = _ffn(hs, ffn1_norm_pre[l], ffn1_norm_post[l], ffn1_w_up[l], ffn1_w_down[l])
        qa, ka, va, qm, km, vm, gates = _in_proj(hs, mix_norm_pre[l], w_in[l])
        o_sb = _sb_sample(qa, ka, va, cache_sb_k, cache_sb_v, page_table, l)
        sel = _moba_select(qm, cache_moba_k, page_table, l, n_sel)[:, :, :n_sel].reshape(db, N_HEADS * n_sel)
        o_m = _moba_sample(qm, km, vm, cache_moba_k, cache_moba_v, page_table, sel, l, n_sel)
        hs = _merge(hs, o_sb.reshape(db, WIDTH), o_m.reshape(db, WIDTH), gates, b_gate[l],
                    w_branch_sb[l], w_branch_moba[l], w_out[l], mix_norm_post[l])
        hs = _ffn(hs, ffn2_norm_pre[l], ffn2_norm_post[l], ffn2_w_up[l], ffn2_w_down[l])
        for acc, r in zip(rows_s, (ka, va, km, vm)):
            acc.append(r.reshape(db, dec_seq, N_HEADS, HEAD_DIM))

    rows_p = [t.reshape(depth, batch, N_HEADS, HEAD_DIM, seq).transpose(0, 1, 4, 2, 3) for t in carried]
    return (hp.reshape(batch, seq, d), hs.reshape(db, dec_seq, d),
            *rows_p, *(jnp.stack(r) for r in rows_s))
```

```python
import functools

import numpy as np
import jax
import jax.numpy as jnp
from jax import lax
from jax.experimental import pallas as pl
from jax.experimental.pallas import tpu as pltpu

HEAD_DIM = 64
N_HEADS = 8
WIDTH = N_HEADS * HEAD_DIM
PAIR = 2 * HEAD_DIM
N_PAIRS = N_HEADS // 2
MOBA_BLOCK = 256
MOBA_TOPK = 3
PAGE_SIZE = 128
NORM_EPS = 1e-6
LANES = 128
VMEM_LIMIT = 56 * 1024 * 1024

F32 = jnp.float32
BF16 = jnp.bfloat16
NEG_INF = float("-inf")
NT_DIMS = (((1,), (1,)), ((), ()))


def _params(*semantics):
    return pltpu.CompilerParams(dimension_semantics=semantics, vmem_limit_bytes=VMEM_LIMIT)


def _rms(x, g):
    return x * lax.rsqrt(jnp.mean(x * x, axis=-1, keepdims=True) + NORM_EPS) * g


def _resident(shape, index_map):
    return pl.BlockSpec(shape, index_map, pipeline_mode=pl.Buffered(1))


FFN_CHUNK = 256


def _dot(a, b, dims=None):
    precision = lax.Precision.HIGHEST if a.dtype == F32 else None
    if dims is None:
        return jnp.dot(a, b, preferred_element_type=F32, precision=precision)
    return lax.dot_general(a, b, dims, preferred_element_type=F32, precision=precision)


def _ffn_kernel(x_ref, gpre_ref, gpost_ref, wa_ref, wb_ref, wd_ref, o_ref, acc_ref, u_ref):
    j = pl.program_id(1)

    @pl.when(j == 0)
    def _():
        u_ref[...] = _rms(x_ref[...], gpre_ref[...]).astype(u_ref.dtype)
        acc_ref[...] = jnp.zeros_like(acc_ref)

    u = u_ref[...]
    for c in range(wd_ref.shape[0] // FFN_CHUNK):
        cols = slice(c * FFN_CHUNK, (c + 1) * FFN_CHUNK)
        a = _dot(u, wa_ref[:, cols])
        b = _dot(u, wb_ref[:, cols])
        mid = (a * jax.nn.sigmoid(a) * b).astype(u.dtype)
        acc_ref[...] += _dot(mid, wd_ref[cols, :])

    @pl.when(j == pl.num_programs(1) - 1)
    def _():
        o_ref[...] = x_ref[...] + 0.5 * _rms(acc_ref[...], gpost_ref[...])


def _ffn(x, g_pre, g_post, w_up, w_down):
    m, d = x.shape
    f = w_down.shape[0]
    assert f % FFN_CHUNK == 0 and w_up.shape == (d, 2 * f)
    tm = min(1024, m)
    assert m % tm == 0
    if w_up.dtype == BF16:
        tf, spec = f, _resident
    else:
        tf, spec = FFN_CHUNK, pl.BlockSpec
    n_slabs = f // tf
    return pl.pallas_call(
        _ffn_kernel,
        grid=(m // tm, n_slabs),
        in_specs=[
            pl.BlockSpec((tm, d), lambda i, j: (i, 0)),
            _resident((1, d), lambda i, j: (0, 0)),
            _resident((1, d), lambda i, j: (0, 0)),
            spec((d, tf), lambda i, j: (0, j)),
            spec((d, tf), lambda i, j: (0, n_slabs + j)),
            spec((tf, d), lambda i, j: (j, 0)),
        ],
        out_specs=pl.BlockSpec((tm, d), lambda i, j: (i, 0)),
        out_shape=jax.ShapeDtypeStruct((m, d), F32),
        scratch_shapes=[pltpu.VMEM((tm, d), F32), pltpu.VMEM((tm, d), w_up.dtype)],
        compiler_params=_params("parallel", "arbitrary"),
        name="ffn",
    )(x, g_pre.reshape(1, d), g_post.reshape(1, d), w_up, w_up, w_down)


def _in_kernel(*refs, prompt, n_carried):
    h_ref, g_ref, w_ref = refs[:3]
    outs = refs[3 + n_carried:]
    qa_ref, ka_ref, va_ref, qm_ref, km_ref, vm_ref, gate_ref = outs[:7]
    if prompt:
        kmean_ref, kat_ref, vat_ref, kmt_ref, vmt_ref = outs[7:]
    else:
        kat_ref = vat_ref = kmt_ref = vmt_ref = None
    u = _rms(h_ref[...], g_ref[...]).astype(w_ref.dtype)
    scale = HEAD_DIM ** -0.5

    def proj(c):
        return _dot(u, w_ref[:, c * WIDTH:(c + 1) * WIDTH])

    def emit(c, row_ref, t_ref):
        r = proj(c)
        row_ref[...] = r.astype(row_ref.dtype)
        if prompt:
            t_ref[...] = r.T
        return r

    qa_ref[...] = (proj(0) * scale).astype(qa_ref.dtype)
    emit(1, ka_ref, kat_ref)
    emit(2, va_ref, vat_ref)
    qm_ref[...] = (proj(3) * scale).astype(qm_ref.dtype)
    r = emit(4, km_ref, kmt_ref)
    if prompt:
        for j in range(r.shape[0] // MOBA_BLOCK):
            blk = r[j * MOBA_BLOCK:(j + 1) * MOBA_BLOCK]
            kmean_ref[j] = jnp.sum(blk, axis=0, keepdims=True) * (1.0 / MOBA_BLOCK)
    emit(5, vm_ref, vmt_ref)
    for c in range(gate_ref.shape[1] // WIDTH):
        gate_ref[:, c * WIDTH:(c + 1) * WIDTH] = proj(6 + c)


def _in_proj(h, g, w_in, *, layer=0, depth=1, batch=None, carried=()):
    m, d = h.shape
    n_cols = w_in.shape[1]
    assert n_cols == 6 * WIDTH + 2 * d
    prompt = batch is not None
    tm = min(512, m)
    assert m % tm == 0
    row = lambda i: (i, 0)
    wide = pl.BlockSpec((tm, WIDTH), row)
    kv_dtype = BF16 if prompt else F32
    out_specs = [wide] * 6 + [pl.BlockSpec((tm, 2 * d), row)]
    out_shape = [jax.ShapeDtypeStruct((m, WIDTH), kv_dtype)] * 6
    out_shape.append(jax.ShapeDtypeStruct((m, 2 * d), F32))
    aliases = {}
    if prompt:
        seq = m // batch
        assert seq % tm == 0 and tm % MOBA_BLOCK == 0
        tiles = seq // tm
        out_specs.append(pl.BlockSpec((tm // MOBA_BLOCK, 1, WIDTH), lambda i: (i, 0, 0)))
        out_shape.append(jax.ShapeDtypeStruct((m // MOBA_BLOCK, 1, WIDTH), F32))
        out_specs += [pl.BlockSpec((None, None, WIDTH, tm), lambda i: (layer, i // tiles, 0, i % tiles))] * 4
        out_shape += [jax.ShapeDtypeStruct((depth, batch, WIDTH, seq), F32)] * 4
        assert len(carried) == 4
        aliases = {3 + j: 8 + j for j in range(4)}
    return pl.pallas_call(
        functools.partial(_in_kernel, prompt=prompt, n_carried=len(carried)),
        grid=(m // tm,),
        in_specs=[
            pl.BlockSpec((tm, d), row),
            _resident((1, d), lambda i: (0, 0)),
            _resident((d, n_cols), lambda i: (0, 0)),
        ] + [pl.BlockSpec(memory_space=pl.ANY)] * len(carried),
        out_specs=out_specs,
        out_shape=out_shape,
        input_output_aliases=aliases,
        compiler_params=_params("parallel"),
        name="in_proj",
    )(h, g.reshape(1, d), w_in, *carried)


def _merge_kernel(h_ref, osb_ref, om_ref, gate_ref, bg_ref, wbs_ref, wbm_ref, wo_ref, gpost_ref, o_ref):
    d = h_ref.shape[1]
    a = _dot(osb_ref[...], wbs_ref[...])
    b = _dot(om_ref[...], wbm_ref[...])
    merged = (jax.nn.sigmoid(gate_ref[:, :d] + bg_ref[:, :d]) * a
              + jax.nn.sigmoid(gate_ref[:, d:] + bg_ref[:, d:]) * b)
    y = _dot(merged.astype(wo_ref.dtype), wo_ref[...])
    o_ref[...] = h_ref[...] + _rms(y, gpost_ref[...])


def _merge(h, o_sb, o_m, gates, b_gate, w_bs, w_bm, w_out, g_post):
    m, d = h.shape
    tm = min(512, m)
    row = lambda i: (i, 0)
    const = lambda i: (0, 0)
    return pl.pallas_call(
        _merge_kernel,
        grid=(m // tm,),
        in_specs=[
            pl.BlockSpec((tm, d), row),
            pl.BlockSpec((tm, WIDTH), row),
            pl.BlockSpec((tm, WIDTH), row),
            pl.BlockSpec((tm, 2 * d), row),
            _resident((1, 2 * d), const),
            _resident((WIDTH, d), const),
            _resident((WIDTH, d), const),
            _resident((d, d), const),
            _resident((1, d), const),
        ],
        out_specs=pl.BlockSpec((tm, d), row),
        out_shape=jax.ShapeDtypeStruct((m, d), F32),
        compiler_params=_params("parallel"),
        name="merge",
    )(h, o_sb, o_m, gates, b_gate.reshape(1, 2 * d), w_bs, w_bm, w_out, g_post.reshape(1, d))


def _suffix_matrix(n):
    j = np.arange(n)[:, None]
    s = np.arange(n)[None, :]
    u1 = np.concatenate([(j > s), np.ones((n, n), bool)], axis=1)
    return jnp.asarray(np.concatenate([u1, u1], axis=0), dtype=BF16)


SB_DEAD = -120.0


def _log_sigmoids(z):
    log_beta = jnp.minimum(z, 0.0) - jnp.log(1.0 + jnp.exp(-jnp.abs(z)))
    return log_beta, log_beta - z


def _suffix_sums(x, uo):
    hi = x.astype(BF16)
    lo = (x - hi.astype(F32)).astype(BF16)
    t = jnp.dot(jnp.concatenate([hi, lo], axis=1), uo, preferred_element_type=F32)
    n = x.shape[1]
    return t[:, :n], t[:, n:]


SB_Q_TILE = 256
SB_K_TILE = 128


def _sb_prompt_kernel(q_ref, k_ref, v_ref, uo_ref, o_ref, carry_ref, acc_ref):
    i = pl.program_id(2)
    tq, tk = SB_Q_TILE, SB_K_TILE
    per = tq // tk
    q = q_ref[...]
    uo = uo_ref[...]
    head0 = (lax.broadcasted_iota(jnp.int32, (1, PAIR), 1) // HEAD_DIM) == 0
    row = lax.broadcasted_iota(jnp.int32, (tq, tk), 0)
    col = lax.broadcasted_iota(jnp.int32, (tq, tk), 1)
    carry_ref[...] = jnp.zeros_like(carry_ref)
    acc_ref[...] = jnp.zeros_like(acc_ref)

    def by_head(x):
        zero = jnp.zeros_like(x)
        return jnp.concatenate([jnp.where(head0, x, zero), jnp.where(head0, zero, x)], axis=0)

    def sweep(key_tiles, masks):
        pending = []
        for jt, mask in zip(key_tiles, masks):
            off = pl.multiple_of(jt * tk, tk)
            z2 = lax.dot_general(q, by_head(k_ref[pl.ds(off, tk), :]), NT_DIMS, preferred_element_type=F32)
            partial, total = [], []
            for hh in range(2):
                log_beta, log_keep = _log_sigmoids(z2[:, hh * tk:(hh + 1) * tk])
                if mask is not None:
                    log_keep = jnp.where(mask, log_keep, 0.0)
                later, tot = _suffix_sums(log_keep, uo)
                partial.append(log_beta + later)
                total.append(tot)
            pending.append((off, mask, partial, total))
        for off, mask, partial, total in pending:
            weights = []
            for hh in range(2):
                carry = carry_ref[hh]
                a = jnp.exp(partial[hh] + carry)
                if mask is not None:
                    a = jnp.where(mask, a, 0.0)
                weights.append(a.astype(BF16))
                carry_ref[hh] = carry + total[hh]
            acc_ref[...] += jnp.dot(jnp.concatenate(weights, axis=1), by_head(v_ref[pl.ds(off, tk), :]),
                                    preferred_element_type=F32)

    inner = list(reversed(range(per)))
    sweep([i * per + d for d in inner], [col + d * tk < row for d in inner])

    def past(state):
        s, _ = state
        top = (i - s) * per - 1
        sweep([top - d for d in range(per)], [None] * per)
        alive = jnp.max(jnp.maximum(carry_ref[0], carry_ref[1])) > SB_DEAD
        return s + 1, alive.astype(jnp.int32)

    lax.while_loop(lambda st: (st[0] < i) & (st[1] > 0), past, (jnp.int32(0), jnp.int32(1)))
    o_ref[...] = acc_ref[...].astype(BF16)


def _sb_prompt(q, kb, vb, batch, seq):
    tq, tk = SB_Q_TILE, SB_K_TILE
    nq = seq // tq
    assert seq % tq == 0
    return pl.pallas_call(
        _sb_prompt_kernel,
        grid=(batch, N_PAIRS, nq),
        in_specs=[
            pl.BlockSpec((tq, PAIR), lambda b, p, i: (b * nq + i, p)),
            pl.BlockSpec((seq, PAIR), lambda b, p, i: (b, p)),
            pl.BlockSpec((seq, PAIR), lambda b, p, i: (b, p)),
            _resident((2 * tk, 2 * tk), lambda b, p, i: (0, 0)),
        ],
        out_specs=pl.BlockSpec((tq, PAIR), lambda b, p, i: (b * nq + i, p)),
        out_shape=jax.ShapeDtypeStruct((batch * seq, WIDTH), BF16),
        scratch_shapes=[pltpu.VMEM((2, tq, tk), F32), pltpu.VMEM((tq, PAIR), F32)],
        compiler_params=_params("parallel", "parallel", "parallel"),
        name="sb_prompt",
    )(q, kb, vb, _suffix_matrix(tk))


def _alibi_slopes():
    return jnp.asarray(2.0 ** (-8.0 * np.arange(1, N_HEADS + 1) / N_HEADS), dtype=F32)


MASKED = -1e30


def _moba_prompt_kernel(slopes_ref, q_ref, k_ref, v_ref, kma_ref, kmb_ref, o_ref,
                        s_ref, mx_ref, ls_ref, acc_ref, *, nb):
    p = pl.program_id(1)
    i = pl.program_id(2)
    t = MOBA_BLOCK
    q = q_ref[...]
    lane = lax.broadcasted_iota(jnp.int32, (1, PAIR), 1)
    row = lax.broadcasted_iota(jnp.int32, (t, t), 0)
    col = lax.broadcasted_iota(jnp.int32, (t, t), 1)
    row_minus_col = (row - col).astype(F32)
    causal = col <= row
    pl_lane = lax.broadcasted_iota(jnp.int32, (t, LANES), 1)
    m_of = pl_lane // nb
    n_of = pl_lane % nb
    pair_ok = (pl_lane < nb * nb) & (m_of < i)
    gl = lax.broadcasted_iota(jnp.int32, (LANES, LANES), 0)
    gn = lax.broadcasted_iota(jnp.int32, (LANES, LANES), 1)
    group = jnp.where((gl < nb * nb) & (gl % nb == gn), 1.0, 0.0).astype(BF16)
    pick_row = lax.broadcasted_iota(jnp.int32, (LANES, t), 0)
    qh, slope, unselected = [], [], []
    for hh in range(2):
        qh.append(jnp.where((lane // HEAD_DIM) == hh, q, jnp.zeros_like(q)))
        slope.append(slopes_ref[2 * p + hh])
        sa = lax.dot_general(qh[hh], kma_ref[...], NT_DIMS, preferred_element_type=F32)
        sb = lax.dot_general(qh[hh], kmb_ref[...], NT_DIMS, preferred_element_type=F32)
        beats = ((sa > sb) | ((sa == sb) & (m_of < n_of))) & pair_ok
        rank = jnp.dot(jnp.where(beats, 1.0, 0.0).astype(BF16), group, preferred_element_type=F32)
        unselected.append(jnp.where((rank < MOBA_TOPK) & (pl_lane < i), 0.0, 1.0).astype(BF16))

    def logits(hh, n):
        off = pl.multiple_of(n * t, t)
        z = lax.dot_general(qh[hh], k_ref[pl.ds(off, t), :], NT_DIMS, preferred_element_type=F32)
        return z - slope[hh] * (row_minus_col + ((i - n) * t).astype(F32))

    for hh in range(2):
        z = jnp.where(causal, logits(hh, i), MASKED)
        s_ref[hh, i] = z
        mx_ref[hh] = z

    def scan(n):
        pick = jnp.where(pick_row == n, MASKED, 0.0).astype(BF16)
        for hh in range(2):
            z = logits(hh, n) + jnp.dot(unselected[hh], pick, preferred_element_type=F32)
            s_ref[hh, n] = z
            mx_ref[hh] = jnp.maximum(mx_ref[hh], z)

    def in_pairs(count, block):
        def two(j, _):
            block(2 * j)
            block(2 * j + 1)
            return 0

        lax.fori_loop(0, count // 2, two, 0)

        @pl.when(count % 2 == 1)
        def _():
            block(count - 1)

    in_pairs(i, scan)

    m = [jnp.max(mx_ref[hh], axis=-1, keepdims=True) for hh in range(2)]
    ls_ref[...] = jnp.zeros_like(ls_ref)
    acc_ref[...] = jnp.zeros_like(acc_ref)

    def accumulate(n):
        v = v_ref[pl.ds(pl.multiple_of(n * t, t), t), :]
        for hh in range(2):
            e = jnp.exp(s_ref[hh, n] - m[hh])
            ls_ref[hh] += e
            acc_ref[hh] += jnp.dot(e.astype(BF16), v, preferred_element_type=F32)

    in_pairs(i + 1, accumulate)
    o = [acc_ref[hh] / jnp.sum(ls_ref[hh], axis=-1, keepdims=True) for hh in range(2)]
    o_ref[...] = jnp.where((lane // HEAD_DIM) == 0, o[0], o[1]).astype(BF16)


def _moba_prompt(q, kb, vb, kmean, batch, seq):
    t = MOBA_BLOCK
    nb = seq // t
    assert seq % t == 0 and nb * nb <= LANES
    km = kmean.reshape(batch, nb, WIDTH).astype(BF16)
    pad = jnp.zeros((batch, LANES - nb * nb, WIDTH), BF16)
    kma = jnp.concatenate([jnp.repeat(km, nb, axis=1), pad], axis=1).reshape(batch * LANES, WIDTH)
    kmb = jnp.concatenate([jnp.tile(km, (1, nb, 1)), pad], axis=1).reshape(batch * LANES, WIDTH)
    return pl.pallas_call(
        functools.partial(_moba_prompt_kernel, nb=nb),
        grid_spec=pltpu.PrefetchScalarGridSpec(
            num_scalar_prefetch=1,
            grid=(batch, N_PAIRS, nb),
            in_specs=[
                pl.BlockSpec((t, PAIR), lambda b, p, i, s: (b * nb + i, p)),
                pl.BlockSpec((seq, PAIR), lambda b, p, i, s: (b, p)),
                pl.BlockSpec((seq, PAIR), lambda b, p, i, s: (b, p)),
                pl.BlockSpec((LANES, PAIR), lambda b, p, i, s: (b, p)),
                pl.BlockSpec((LANES, PAIR), lambda b, p, i, s: (b, p)),
            ],
            out_specs=pl.BlockSpec((t, PAIR), lambda b, p, i, s: (b * nb + i, p)),
            scratch_shapes=[
                pltpu.VMEM((2, nb, t, t), F32),
                pltpu.VMEM((2, t, t), F32),
                pltpu.VMEM((2, t, t), F32),
                pltpu.VMEM((2, t, PAIR), F32),
            ],
        ),
        out_shape=jax.ShapeDtypeStruct((batch * seq, WIDTH), BF16),
        compiler_params=_params("parallel", "parallel", "parallel"),
        name="moba_prompt",
    )(_alibi_slopes(), q, kb, vb, kma, kmb)


CHUNK_PAGES = 16
CHUNK_KEYS = CHUNK_PAGES * PAGE_SIZE
SUB = 256


def _page_copies(pt_ref, pool_ref, buf_ref, sem_ref, layer, step, slot, n_chunks, descending):
    b = step // n_chunks
    c = step % n_chunks
    first_page = ((n_chunks - 1 - c) if descending else c) * CHUNK_PAGES
    return [
        pltpu.make_async_copy(pool_ref.at[layer, pt_ref[b, first_page + j]], buf_ref.at[slot, j], sem_ref.at[slot])
        for j in range(CHUNK_PAGES)
    ]


def _stream_chunk(pt_ref, pools_bufs_sems, layer, n_chunks, descending):
    step = pl.program_id(0) * n_chunks + pl.program_id(1)
    n_steps = pl.num_programs(0) * n_chunks
    slot = step % 2

    def start(s, sl):
        for pool_ref, buf_ref, sem_ref in pools_bufs_sems:
            for cp in _page_copies(pt_ref, pool_ref, buf_ref, sem_ref, layer, s, sl, n_chunks, descending):
                cp.start()

    @pl.when(step == 0)
    def _():
        start(step, slot)

    @pl.when(step + 1 < n_steps)
    def _():
        start(step + 1, 1 - slot)

    for pool_ref, buf_ref, sem_ref in pools_bufs_sems:
        for cp in _page_copies(pt_ref, pool_ref, buf_ref, sem_ref, layer, step, slot, n_chunks, descending):
            cp.wait()
    return slot


def _head_rows(x):
    rows = lax.broadcasted_iota(jnp.int32, (N_HEADS, WIDTH), 0)
    lanes = lax.broadcasted_iota(jnp.int32, (N_HEADS, WIDTH), 1)
    xb = jnp.broadcast_to(x.astype(F32), (N_HEADS, WIDTH))
    return jnp.where(lanes // HEAD_DIM == rows, xb, 0.0)


SB_SAMPLE_PAGES = 4
SB_SAMPLE_SLOTS = 4


def _sb_sample_kernel(pt_ref, q_ref, kn_ref, vn_ref, uo_ref, kpool_ref, vpool_ref, o_ref,
                      kbuf, vbuf, ksem, vsem, carry_ref, acc_ref, *, layer, n_chunks):
    b = pl.program_id(0)
    chunk_keys = SB_SAMPLE_PAGES * PAGE_SIZE

    def copies(sample, c):
        slot = jnp.where(c == 0, 2 + sample % 2, c % 2)
        first_page = (n_chunks - 1 - c) * SB_SAMPLE_PAGES
        out = []
        for j in range(SB_SAMPLE_PAGES):
            page = pt_ref[sample, first_page + j]
            out.append(pltpu.make_async_copy(kpool_ref.at[layer, page], kbuf.at[slot, j], ksem.at[slot]))
            out.append(pltpu.make_async_copy(vpool_ref.at[layer, page], vbuf.at[slot, j], vsem.at[slot]))
        return slot, out

    def start(sample, c):
        for cp in copies(sample, c)[1]:
            cp.start()

    def wait(sample, c):
        slot, out = copies(sample, c)
        for cp in out:
            cp.wait()
        return slot

    @pl.when(b == 0)
    def _():
        start(b, 0)

    @pl.when(b + 1 < pl.num_programs(0))
    def _():
        start(b + 1, 0)

    qb = _head_rows(q_ref[...]).astype(BF16)
    uo = uo_ref[...]
    t_new = lax.broadcasted_iota(jnp.int32, (1, 1), 0)
    allowed = t_new < t_new
    z_new = jnp.sum(qb.astype(F32) * kn_ref[...].astype(BF16).astype(F32), axis=-1, keepdims=True)
    log_beta_new, log_keep_new = _log_sigmoids(z_new)
    carry_ref[...] = jnp.broadcast_to(jnp.where(allowed, log_keep_new, 0.0), carry_ref.shape)
    acc_ref[...] = jnp.where(allowed, jnp.exp(log_beta_new), 0.0) * _head_rows(vn_ref[...].astype(BF16).astype(F32))

    def sweep_chunk(state):
        c, _ = state

        @pl.when(c + 1 < n_chunks)
        def _():
            start(b, c + 1)

        slot = wait(b, c)
        z = jnp.concatenate([jnp.dot(qb, kbuf[slot, j].astype(BF16), preferred_element_type=F32)
                             for j in range(SB_SAMPLE_PAGES)], axis=1)
        log_beta, log_keep = _log_sigmoids(z)
        n_sub = chunk_keys // SUB
        sums = [_suffix_sums(log_keep[:, s * SUB:(s + 1) * SUB], uo) for s in range(n_sub)]
        carry = carry_ref[...]
        weights = [None] * n_sub
        for s in reversed(range(n_sub)):
            later, total = sums[s]
            weights[s] = jnp.exp(log_beta[:, s * SUB:(s + 1) * SUB] + later + carry)
            carry = carry + total
        carry_ref[...] = carry
        acc = acc_ref[...]
        pages_per_sub = SUB // PAGE_SIZE
        for j in range(SB_SAMPLE_PAGES):
            w = weights[j // pages_per_sub]
            off = (j % pages_per_sub) * PAGE_SIZE
            acc = acc + lax.dot_general(w[:, off:off + PAGE_SIZE].astype(BF16), vbuf[slot, j].astype(BF16),
                                        NT_DIMS, preferred_element_type=F32)
        acc_ref[...] = acc
        return c + 1, (jnp.max(carry) > SB_DEAD).astype(jnp.int32)

    c_end, _ = lax.while_loop(lambda st: (st[0] < n_chunks) & (st[1] > 0), sweep_chunk,
                              (jnp.int32(0), jnp.int32(1)))

    @pl.when(c_end < n_chunks)
    def _():
        wait(b, c_end)

    rows = lax.broadcasted_iota(jnp.int32, (N_HEADS, WIDTH), 0)
    lanes = lax.broadcasted_iota(jnp.int32, (N_HEADS, WIDTH), 1)
    own = jnp.where(lanes // HEAD_DIM == rows, acc_ref[...], 0.0)
    o_ref[...] = jnp.sum(own, axis=0, keepdims=True)


def _pool_view(pool):
    depth, n_pool, page, heads, dh = pool.shape
    assert page == PAGE_SIZE and heads * dh == WIDTH
    return pool.transpose(0, 1, 3, 4, 2).reshape(depth, n_pool, WIDTH, PAGE_SIZE)


def _sb_sample(q, k_new, v_new, pool_k, pool_v, page_table, layer):
    db, n_pages = page_table.shape
    assert n_pages % SB_SAMPLE_PAGES == 0 and (SB_SAMPLE_PAGES * PAGE_SIZE) % SUB == 0
    n_chunks = n_pages // SB_SAMPLE_PAGES
    row3 = pl.BlockSpec((None, 1, WIDTH), lambda b, pt: (b, 0, 0))
    slots = (SB_SAMPLE_SLOTS, SB_SAMPLE_PAGES, WIDTH, PAGE_SIZE)
    return pl.pallas_call(
        functools.partial(_sb_sample_kernel, layer=layer, n_chunks=n_chunks),
        grid_spec=pltpu.PrefetchScalarGridSpec(
            num_scalar_prefetch=1,
            grid=(db,),
            in_specs=[row3, row3, row3,
                      _resident((2 * SUB, 2 * SUB), lambda b, pt: (0, 0)),
                      pl.BlockSpec(memory_space=pl.ANY),
                      pl.BlockSpec(memory_space=pl.ANY)],
            out_specs=row3,
            scratch_shapes=[
                pltpu.VMEM(slots, F32),
                pltpu.VMEM(slots, F32),
                pltpu.SemaphoreType.DMA((SB_SAMPLE_SLOTS,)),
                pltpu.SemaphoreType.DMA((SB_SAMPLE_SLOTS,)),
                pltpu.VMEM((N_HEADS, SUB), F32),
                pltpu.VMEM((N_HEADS, WIDTH), F32),
            ],
        ),
        out_shape=jax.ShapeDtypeStruct((db, 1, WIDTH), F32),
        compiler_params=_params("arbitrary"),
        name="sb_sample",
    )(page_table, q.reshape(db, 1, WIDTH), k_new.reshape(db, 1, WIDTH), v_new.reshape(db, 1, WIDTH),
      _suffix_matrix(SUB), _pool_view(pool_k), _pool_view(pool_v))


def _moba_select_kernel(pt_ref, q_ref, kpool_ref, sel_ref, kbuf, ksem, kmean_ref, *, layer, n_chunks, n_sel):
    c = pl.program_id(1)
    slot = _stream_chunk(pt_ref, [(kpool_ref, kbuf, ksem)], layer, n_chunks, False)
    pages_per_block = MOBA_BLOCK // PAGE_SIZE
    blocks_per_chunk = CHUNK_PAGES // pages_per_block
    n_blocks = kmean_ref.shape[1]

    @pl.when(c == 0)
    def _():
        kmean_ref[...] = jnp.zeros_like(kmean_ref)

    block_lane = lax.broadcasted_iota(jnp.int32, kmean_ref.shape, 1)
    means = kmean_ref[...]
    for j in range(blocks_per_chunk):
        total = kbuf[slot, j * pages_per_block]
        for t in range(1, pages_per_block):
            total = total + kbuf[slot, j * pages_per_block + t]
        column = jnp.sum(total, axis=-1, keepdims=True) * (1.0 / MOBA_BLOCK)
        means = jnp.where(block_lane == c * blocks_per_chunk + j, column, means)
    kmean_ref[...] = means

    @pl.when(c == n_chunks - 1)
    def _():
        sc = _dot(_head_rows(q_ref[...]), means)
        lane = lax.broadcasted_iota(jnp.int32, sc.shape, 1).astype(F32)
        out_lane = lax.broadcasted_iota(jnp.int32, (N_HEADS, LANES), 1)
        picks = jnp.zeros((N_HEADS, LANES), jnp.int32)
        for r in range(n_sel):
            best = jnp.max(sc, axis=-1, keepdims=True)
            idx = jnp.min(jnp.where(sc == best, lane, float(n_blocks)), axis=-1, keepdims=True)
            picks = jnp.where(out_lane == r, idx.astype(jnp.int32), picks)
            sc = jnp.where(lane == idx, NEG_INF, sc)
        sel_ref[...] = picks


def _moba_select(q, pool_k, page_table, layer, n_sel):
    db, n_pages = page_table.shape
    n_chunks = n_pages // CHUNK_PAGES
    n_blocks = n_pages * PAGE_SIZE // MOBA_BLOCK
    return pl.pallas_call(
        functools.partial(_moba_select_kernel, layer=layer, n_chunks=n_chunks, n_sel=n_sel),
        grid_spec=pltpu.PrefetchScalarGridSpec(
            num_scalar_prefetch=1,
            grid=(db, n_chunks),
            in_specs=[pl.BlockSpec((None, 1, WIDTH), lambda b, c, pt: (b, 0, 0)),
                      pl.BlockSpec(memory_space=pl.ANY)],
            out_specs=pl.BlockSpec((None, N_HEADS, LANES), lambda b, c, pt: (b, 0, 0)),
            scratch_shapes=[
                pltpu.VMEM((2, CHUNK_PAGES, WIDTH, PAGE_SIZE), F32),
                pltpu.SemaphoreType.DMA((2,)),
                pltpu.VMEM((WIDTH, n_blocks), F32),
            ],
        ),
        out_shape=jax.ShapeDtypeStruct((db, N_HEADS, LANES), jnp.int32),
        compiler_params=_params("arbitrary", "arbitrary"),
        name="moba_select",
    )(page_table, q.reshape(db, 1, WIDTH), _pool_view(pool_k))


def _moba_sample_kernel(pt_ref, sel_ref, slopes_ref, q_ref, kn_ref, vn_ref, kpool_ref, vpool_ref, o_ref,
                        kbuf, vbuf, sem, *, layer, n_sel, past):
    b = pl.program_id(0)
    pages_per_block = MOBA_BLOCK // PAGE_SIZE
    n_keys = n_sel * MOBA_BLOCK

    def copies():
        out = []
        for h in range(N_HEADS):
            dims = pl.ds(h * HEAD_DIM, HEAD_DIM)
            for r in range(n_sel):
                blk = sel_ref[b, h * n_sel + r]
                for j in range(pages_per_block):
                    page = pt_ref[b, blk * pages_per_block + j]
                    keys = pl.ds((r * pages_per_block + j) * PAGE_SIZE, PAGE_SIZE)
                    out.append(pltpu.make_async_copy(kpool_ref.at[layer, page, dims], kbuf.at[h, :, keys], sem.at[0]))
                    out.append(pltpu.make_async_copy(vpool_ref.at[layer, page, dims], vbuf.at[h, :, keys], sem.at[1]))
        return out

    for cp in copies():
        cp.start()
    for cp in copies():
        cp.wait()

    col = lax.broadcasted_iota(jnp.int32, (1, n_keys), 1)
    rank_of = col // MOBA_BLOCK
    head_row = lax.broadcasted_iota(jnp.int32, (N_HEADS, HEAD_DIM), 0)
    q_all = q_ref[...].astype(BF16).astype(F32)
    k_new = kn_ref[...].astype(BF16).astype(F32)
    v_new = vn_ref[...].astype(BF16).astype(F32)
    out = jnp.zeros((N_HEADS, HEAD_DIM), F32)
    for h in range(N_HEADS):
        qh = q_all[h:h + 1]
        q_rows = jnp.broadcast_to(qh, (N_HEADS, HEAD_DIM)).astype(BF16)
        slope = slopes_ref[h]
        z = jnp.dot(q_rows, kbuf[h].astype(BF16), preferred_element_type=F32)[0:1]
        blk = jnp.zeros((1, n_keys), jnp.int32)
        for r in range(n_sel):
            blk = jnp.where(rank_of == r, sel_ref[b, h * n_sel + r], blk)
        pos = blk * MOBA_BLOCK + col % MOBA_BLOCK
        z = z - slope * (past - pos).astype(F32)
        z_own = jnp.sum(qh * k_new[h:h + 1], axis=-1, keepdims=True)
        m = jnp.maximum(jnp.max(z, axis=-1, keepdims=True), z_own)
        e = jnp.exp(z - m)
        e_own = jnp.exp(z_own - m)
        denom = jnp.sum(e, axis=-1, keepdims=True) + e_own
        pv = lax.dot_general(jnp.broadcast_to(e, (N_HEADS, n_keys)).astype(BF16), vbuf[h].astype(BF16), NT_DIMS,
                             preferred_element_type=F32)[0:1]
        o_h = (pv + e_own * v_new[h:h + 1]) / denom
        out = jnp.where(head_row == h, jnp.broadcast_to(o_h, out.shape), out)
    o_ref[...] = out


def _moba_sample(q, k_new, v_new, pool_k, pool_v, page_table, sel, layer, n_sel):
    db, n_pages = page_table.shape
    past = n_pages * PAGE_SIZE
    assert past % MOBA_BLOCK == 0, "own block must hold no cached rows"
    heads = pl.BlockSpec((None, N_HEADS, HEAD_DIM), lambda b, pt, s, sl: (b, 0, 0))
    as_heads = lambda x: x.reshape(db, N_HEADS, HEAD_DIM)
    return pl.pallas_call(
        functools.partial(_moba_sample_kernel, layer=layer, n_sel=n_sel, past=past),
        grid_spec=pltpu.PrefetchScalarGridSpec(
            num_scalar_prefetch=3,
            grid=(db,),
            in_specs=[heads, heads, heads,
                      pl.BlockSpec(memory_space=pl.ANY),
                      pl.BlockSpec(memory_space=pl.ANY)],
            out_specs=heads,
            scratch_shapes=[
                pltpu.VMEM((N_HEADS, HEAD_DIM, n_sel * MOBA_BLOCK), F32),
                pltpu.VMEM((N_HEADS, HEAD_DIM, n_sel * MOBA_BLOCK), F32),
                pltpu.SemaphoreType.DMA((2,)),
            ],
        ),
        out_shape=jax.ShapeDtypeStruct((db, N_HEADS, HEAD_DIM), F32),
        compiler_params=_params("arbitrary"),
        name="moba_sample",
    )(page_table, sel, _alibi_slopes(), as_heads(q), as_heads(k_new), as_heads(v_new),
      _pool_view(pool_k), _pool_view(pool_v))


def kernel(x_prompt, x_sample, cache_sb_k, cache_sb_v, cache_moba_k, cache_moba_v, page_table, ffn1_norm_pre, ffn1_norm_post, ffn1_w_up, ffn1_w_down, mix_norm_pre, mix_norm_post, w_in, b_gate, w_branch_sb, w_branch_moba, w_out, ffn2_norm_pre, ffn2_norm_post, ffn2_w_up, ffn2_w_down):
    batch, seq, d = x_prompt.shape
    db, dec_seq, _ = x_sample.shape
    assert dec_seq == 1, "the sample kernels take one new token per sequence"
    depth = w_in.shape[0]
    n_pages = page_table.shape[1]
    n_full = n_pages * PAGE_SIZE // MOBA_BLOCK
    n_sel = min(MOBA_TOPK, n_full)
    assert n_sel >= 1

    hp = x_prompt.reshape(batch * seq, d)
    hs = x_sample.reshape(db * dec_seq, d)
    carried = [jnp.zeros((depth, batch, WIDTH, seq), F32) for _ in range(4)]
    rows_s = [[], [], [], []]
    for l in range(depth):
        w1u, w1d = ffn1_w_up[l].astype(BF16), ffn1_w_down[l].astype(BF16)
        w2u, w2d = ffn2_w_up[l].astype(BF16), ffn2_w_down[l].astype(BF16)
        wi = w_in[l].astype(BF16)
        wbs, wbm, wo = w_branch_sb[l].astype(BF16), w_branch_moba[l].astype(BF16), w_out[l].astype(BF16)

        hp = _ffn(hp, ffn1_norm_pre[l], ffn1_norm_post[l], w1u, w1d)
        qa, kab, vab, qm, kmb, vmb, gates, kmean, *carried = _in_proj(
            hp, mix_norm_pre[l], wi, layer=l, depth=depth, batch=batch, carried=tuple(carried))
        o_sb = _sb_prompt(qa, kab, vab, batch, seq)
        o_m = _moba_prompt(qm, kmb, vmb, kmean, batch, seq)
        hp = _merge(hp, o_sb, o_m, gates, b_gate[l], wbs, wbm, wo, mix_norm_post[l])
        hp = _ffn(hp, ffn2_norm_pre[l], ffn2_norm_post[l], w2u, w2d)

        hs = _ffn(hs, ffn1_norm_pre[l], ffn1_norm_post[l], ffn1_w_up[l], ffn1_w_down[l])
        qa, ka, va, qm, km, vm, gates = _in_proj(hs, mix_norm_pre[l], w_in[l])
        o_sb = _sb_sample(qa, ka, va, cache_sb_k, cache_sb_v, page_table, l)
        sel = _moba_select(qm, cache_moba_k, page_table, l, n_sel)[:, :, :n_sel].reshape(db, N_HEADS * n_sel)
        o_m = _moba_sample(qm, km, vm, cache_moba_k, cache_moba_v, page_table, sel, l, n_sel)
        hs = _merge(hs, o_sb.reshape(db, WIDTH), o_m.reshape(db, WIDTH), gates, b_gate[l],
                    w_branch_sb[l], w_branch_moba[l], w_out[l], mix_norm_post[l])
        hs = _ffn(hs, ffn2_norm_pre[l], ffn2_norm_post[l], ffn2_w_up[l], ffn2_w_down[l])
        for acc, r in zip(rows_s, (ka, va, km, vm)):
            acc.append(r.reshape(db, dec_seq, N_HEADS, HEAD_DIM))

    rows_p = [t.reshape(depth, batch, N_HEADS, HEAD_DIM, seq).transpose(0, 1, 4, 2, 3) for t in carried]
    return (hp.reshape(batch, seq, d), hs.reshape(db, dec_seq, d),
            *rows_p, *(jnp.stack(r) for r in rows_s))
```
